```python
import jax, jax.numpy as jnp
from jax import lax
import numpy as np

D_MODEL = 1024
BATCH = 1
SEQ = 16384
DEPTH = 1

GRID_W = 64
CTX_LEN = 256
N_HEADS = 8
N_KV_HEADS = 2
HEAD_DIM = 128
ROPE_THETA = 10000.0
Q_BLOCK = 128
CONV_DIM = 512
CONV_WIDTH = 31
FFN_DIM = 2816
FFN_CONV_WIDTH = 3
EPS = 1e-6
Q_DIM = N_HEADS * HEAD_DIM
KV_DIM = N_KV_HEADS * HEAD_DIM
IN_DIM = Q_DIM + 2 * KV_DIM + 2 * CONV_DIM + 2 * D_MODEL
SPLITS = (Q_DIM, Q_DIM + KV_DIM, Q_DIM + 2 * KV_DIM, Q_DIM + 2 * KV_DIM + 2 * CONV_DIM)

kernel_name = 'hybrid_gqa_conformer_convglu_dit_block'


def rms_norm(x, g):
    xf = x.astype(jnp.float32)
    y = xf * lax.rsqrt(jnp.mean(xf * xf, axis=-1, keepdims=True) + EPS)
    return (y * g.astype(jnp.float32)).astype(x.dtype)


def layer_norm(x, g, b):
    xf = x.astype(jnp.float32)
    mu = jnp.mean(xf, axis=-1, keepdims=True)
    var = jnp.mean(jnp.square(xf - mu), axis=-1, keepdims=True)
    y = (xf - mu) * lax.rsqrt(var + EPS)
    return (y * g.astype(jnp.float32) + b.astype(jnp.float32)).astype(x.dtype)


def depthwise_conv(x, w, b):
    C = x.shape[-1]
    y = lax.conv_general_dilated(x, w[:, None, :], window_strides=(1,), padding='SAME',
                                 dimension_numbers=('NWC', 'WIO', 'NWC'), feature_group_count=C)
    return y + b


def modulation(cvec, w_mod, b_mod):
    m = jax.nn.silu(cvec) @ w_mod + b_mod
    return jnp.split(m[:, None, :], 6, axis=-1)


def modulate(h, shift, scale):
    return h * (1.0 + scale) + shift


def axial_rope(rows):
    half = HEAD_DIM // 2
    inv_freq = ROPE_THETA ** (-jnp.arange(0, half, 2, dtype=jnp.float32) / half)
    r = jnp.repeat(jnp.arange(rows, dtype=jnp.float32), GRID_W)
    col = jnp.tile(jnp.arange(GRID_W, dtype=jnp.float32), rows)
    ang = jnp.concatenate([r[:, None] * inv_freq, col[:, None] * inv_freq], axis=-1)
    return jnp.cos(ang), jnp.sin(ang)


def apply_rope(x, cos, sin):
    xf = x.astype(jnp.float32).reshape(x.shape[:-1] + (HEAD_DIM // 2, 2))
    x0, x1 = xf[..., 0], xf[..., 1]
    c = cos[None, :, None, :]
    s = sin[None, :, None, :]
    out = jnp.stack([x0 * c - x1 * s, x0 * s + x1 * c], axis=-1)
    return out.reshape(x.shape).astype(x.dtype)


def head_rms(t, n_heads, g):
    B, L, _ = t.shape
    return rms_norm(t.reshape(B, L, n_heads, HEAD_DIM), g)


def project(h, w_in, q_g, k_g):
    p = h @ w_in
    q, k, v, u, g = jnp.split(p, SPLITS, axis=-1)
    B, L, _ = h.shape
    q = head_rms(q, N_HEADS, q_g)
    k = head_rms(k, N_KV_HEADS, k_g)
    v = v.reshape(B, L, N_KV_HEADS, HEAD_DIM)
    return q, k, v, u, g


def context_kv(hc, w_in, k_g):
    p = hc @ w_in[:, Q_DIM:Q_DIM + 2 * KV_DIM]
    k, v = jnp.split(p, 2, axis=-1)
    B, L, _ = hc.shape
    return head_rms(k, N_KV_HEADS, k_g), v.reshape(B, L, N_KV_HEADS, HEAD_DIM)


def block_attention(q, k, v):
    B, Lq, H, d = q.shape
    G = H // N_KV_HEADS
    nblk = Lq // Q_BLOCK
    qb = q.reshape(B, nblk, Q_BLOCK, N_KV_HEADS, G, d).transpose(1, 0, 2, 3, 4, 5)
    scale = HEAD_DIM ** -0.5

    def one_block(qblk):
        s = jnp.einsum('bqkgd,bskd->bkgqs', qblk, k, preferred_element_type=jnp.float32) * scale
        p = jax.nn.softmax(s, axis=-1)
        return jnp.einsum('bkgqs,bskd->bqkgd', p.astype(v.dtype), v)

    o = lax.map(one_block, qb)
    return o.transpose(1, 0, 2, 3, 4, 5).reshape(B, Lq, H * d)


def conformer_conv(u, dw_w, dw_b, ln_g, ln_b, w_conv_out):
    a, b = jnp.split(u, 2, axis=-1)
    h = a * jax.nn.sigmoid(b)
    h = depthwise_conv(h, dw_w, dw_b)
    h = jax.nn.silu(layer_norm(h, ln_g, ln_b))
    return h @ w_conv_out


def merge_branches(attn, u, g, w_attn_out, dw_w, dw_b, ln_g, ln_b, w_conv_out, w_out):
    y_a = attn @ w_attn_out
    y_b = conformer_conv(u, dw_w, dw_b, ln_g, ln_b, w_conv_out)
    g_a, g_b = jnp.split(g, 2, axis=-1)
    return (jax.nn.sigmoid(g_a) * y_a + jax.nn.sigmoid(g_b) * y_b) @ w_out


def conv_glu_ffn(h, w_up, dw_w, dw_b, w_down):
    up = h @ w_up
    a, b = jnp.split(up, 2, axis=-1)
    a = depthwise_conv(a, dw_w, dw_b)
    return (jax.nn.gelu(a, approximate=True) * b) @ w_down


def setup_inputs(seed: int = 0) -> dict:
    key = jax.random.key(seed)
    ks = jax.random.split(key, 24)

    def nrm(k, shape, scale):
        return jax.random.normal(k, shape, jnp.float32) * scale

    return {
        'x': nrm(ks[0], (BATCH, SEQ, D_MODEL), 1.0),
        'c': nrm(ks[1], (BATCH, D_MODEL), 1.0),
        'ctx': nrm(ks[2], (BATCH, CTX_LEN, D_MODEL), 1.0),
        'c_ctx': nrm(ks[3], (D_MODEL,), 1.0),
        'w_mod': nrm(ks[4], (DEPTH, D_MODEL, 6 * D_MODEL), 0.5 * D_MODEL ** -0.5),
        'b_mod': nrm(ks[5], (DEPTH, 6 * D_MODEL), 0.02),
        'norm1_g': 1.0 + nrm(ks[6], (DEPTH, D_MODEL), 0.02),
        'w_in': nrm(ks[7], (DEPTH, D_MODEL, IN_DIM), D_MODEL ** -0.5),
        'q_norm_g': 1.0 + nrm(ks[8], (DEPTH, HEAD_DIM), 0.02),
        'k_norm_g': 1.0 + nrm(ks[9], (DEPTH, HEAD_DIM), 0.02),
        'w_attn_out': nrm(ks[10], (DEPTH, Q_DIM, D_MODEL), Q_DIM ** -0.5),
        'conv_dw_w': nrm(ks[11], (DEPTH, CONV_WIDTH, CONV_DIM), CONV_WIDTH ** -0.5),
        'conv_dw_b': nrm(ks[12], (DEPTH, CONV_DIM), 0.02),
        'conv_ln_g': 1.0 + nrm(ks[13], (DEPTH, CONV_DIM), 0.02),
        'conv_ln_b': nrm(ks[14], (DEPTH, CONV_DIM), 0.02),
        'w_conv_out': nrm(ks[15], (DEPTH, CONV_DIM, D_MODEL), CONV_DIM ** -0.5),
        'w_out': nrm(ks[16], (DEPTH, D_MODEL, D_MODEL), D_MODEL ** -0.5),
        'norm2_g': 1.0 + nrm(ks[17], (DEPTH, D_MODEL), 0.02),
        'w_up': nrm(ks[18], (DEPTH, D_MODEL, 2 * FFN_DIM), D_MODEL ** -0.5),
        'ffn_dw_w': nrm(ks[19], (DEPTH, FFN_CONV_WIDTH, FFN_DIM), FFN_CONV_WIDTH ** -0.5),
        'ffn_dw_b': nrm(ks[20], (DEPTH, FFN_DIM), 0.02),
        'w_down': nrm(ks[21], (DEPTH, FFN_DIM, D_MODEL), FFN_DIM ** -0.5),
        'final_g': 1.0 + nrm(ks[22], (D_MODEL,), 0.02),
    }


def reference(x, c, ctx, c_ctx, w_mod, b_mod, norm1_g, w_in, q_norm_g, k_norm_g, w_attn_out,
              conv_dw_w, conv_dw_b, conv_ln_g, conv_ln_b, w_conv_out, w_out, norm2_g,
              w_up, ffn_dw_w, ffn_dw_b, w_down, final_g):
    B, L, _ = x.shape
    rows = L // GRID_W
    cos, sin = axial_rope(rows)
    for i in range(DEPTH):
        sh1, sc1, g1, sh2, sc2, g2 = modulation(c, w_mod[i], b_mod[i])
        csh1, csc1, cg1, csh2, csc2, cg2 = modulation(c_ctx[None], w_mod[i], b_mod[i])

        hx = modulate(rms_norm(x, norm1_g[i]), sh1, sc1)
        hc = modulate(rms_norm(ctx, norm1_g[i]), csh1, csc1)
        qx, kx, vx, ux, gx = project(hx, w_in[i], q_norm_g[i], k_norm_g[i])
        qx = apply_rope(qx, cos, sin)
        kx = apply_rope(kx, cos, sin)
        last = i == DEPTH - 1
        if last:
            kc, vc = context_kv(hc, w_in[i], k_norm_g[i])
        else:
            qc, kc, vc, uc, gc = project(hc, w_in[i], q_norm_g[i], k_norm_g[i])
        attn_x = block_attention(qx, jnp.concatenate([kc, kx], axis=1), jnp.concatenate([vc, vx], axis=1))
        mix_x = merge_branches(attn_x, ux, gx, w_attn_out[i], conv_dw_w[i], conv_dw_b[i],
                               conv_ln_g[i], conv_ln_b[i], w_conv_out[i], w_out[i])
        x = x + g1 * mix_x
        hx2 = modulate(rms_norm(x, norm2_g[i]), sh2, sc2)
        x = x + g2 * conv_glu_ffn(hx2, w_up[i], ffn_dw_w[i], ffn_dw_b[i], w_down[i])

        if not last:
            attn_c = block_attention(qc, kc, vc)
            mix_c = merge_branches(attn_c, uc, gc, w_attn_out[i], conv_dw_w[i], conv_dw_b[i],
                                   conv_ln_g[i], conv_ln_b[i], w_conv_out[i], w_out[i])
            ctx = ctx + cg1 * mix_c
            hc2 = modulate(rms_norm(ctx, norm2_g[i]), csh2, csc2)
            ctx = ctx + cg2 * conv_glu_ffn(hc2, w_up[i], ffn_dw_w[i], ffn_dw_b[i], w_down[i])
    return rms_norm(x, final_g)
```

```python
import functools

import jax
import jax.numpy as jnp
from jax import lax
from jax.experimental import pallas as pl
from jax.experimental.pallas import tpu as pltpu

D_MODEL = 1024
SEQ = 16384
GRID_W = 64
CTX_LEN = 256
N_HEADS = 8
N_KV_HEADS = 2
HEAD_DIM = 128
ROPE_THETA = 10000.0
CONV_DIM = 512
CONV_WIDTH = 31
FFN_DIM = 2816
EPS = 1e-6
Q_DIM = N_HEADS * HEAD_DIM
KV_DIM = N_KV_HEADS * HEAD_DIM
GROUP = N_HEADS // N_KV_HEADS
N_KEYS = SEQ + CTX_LEN
U_OFF = Q_DIM + 2 * KV_DIM
G_OFF = U_OFF + 2 * CONV_DIM
IN_DIM = G_OFF + 2 * D_MODEL

F32 = jnp.float32
BF16 = jnp.bfloat16

LANES = 128
SUBLANES_F32 = 8
SUBLANES_BF16 = 16
VMEM_LIMIT_BYTES = 56 * 1024 * 1024

TM_PROJ = 512
TQ = 512
TK = 1280
N_KCHUNK = N_KEYS // TK
V_ROWS = HEAD_DIM + SUBLANES_BF16
TM_MIX = 512
CONV_HALO = 16
TM_FFN = 512
FFN_HALO = SUBLANES_F32
FFN_CHUNK = 256
N_FCHUNK = FFN_DIM // FFN_CHUNK

assert N_KEYS % TK == 0 and SEQ % TQ == 0 and SEQ % TM_PROJ == 0
assert FFN_DIM % FFN_CHUNK == 0 and CONV_HALO >= CONV_WIDTH // 2


def _dot(a, b):
    return jnp.dot(a, b, preferred_element_type=F32)


def _rms(x):
    return x * lax.rsqrt(jnp.mean(x * x, axis=-1, keepdims=True) + EPS)


def _const_spec(shape):
    return pl.BlockSpec(shape, lambda *_: (0,) * len(shape))


def _params(sem):
    return pltpu.CompilerParams(dimension_semantics=sem, vmem_limit_bytes=VMEM_LIMIT_BYTES)


def _mod_kernel(c_ref, w_ref, b_ref, o_ref):
    c = c_ref[...]
    s = c * jax.nn.sigmoid(c)
    w = w_ref[...]
    s_hi = s.astype(BF16)
    s_lo = (s - s_hi.astype(F32)).astype(BF16)
    w_hi = w.astype(BF16)
    w_lo = (w - w_hi.astype(F32)).astype(BF16)
    o_ref[...] = _dot(s_hi, w_hi) + _dot(s_hi, w_lo) + _dot(s_lo, w_hi) + b_ref[...]


def _modulation(cvecs, w_mod, b_mod):
    rows, n_out = cvecs.shape[0], w_mod.shape[1]
    bn = 1024
    return pl.pallas_call(
        _mod_kernel,
        out_shape=jax.ShapeDtypeStruct((rows, n_out), F32),
        grid=(n_out // bn,),
        in_specs=[pl.BlockSpec((rows, D_MODEL), lambda j: (0, 0)),
                  pl.BlockSpec((D_MODEL, bn), lambda j: (0, j)),
                  pl.BlockSpec((1, bn), lambda j: (0, j))],
        out_specs=pl.BlockSpec((rows, bn), lambda j: (0, j)),
        compiler_params=_params(("arbitrary",)),
        name="modulation",
    )(cvecs, w_mod, b_mod)


def _head_norm(t, gain):
    return _rms(t) * gain


def _rope(t, cos2, sin2, even):
    swapped = jnp.where(even, pltpu.roll(t, HEAD_DIM - 1, 1), pltpu.roll(t, 1, 1))
    return t * cos2 + swapped * sin2


def _l2(tb):
    t = tb.astype(F32)
    return jnp.sqrt(jnp.sum(t * t, axis=-1, keepdims=True))


def _inproj_kernel(x_ref, g_ref, sc_ref, sh_ref, w_ref, qg_ref, kg_ref, cos_ref, sin_ref,
                   *out_refs, full):
    if full:
        q_ref, k_ref, vt_ref, h_ref, sg_ref, qn_ref, kn_ref = out_refs
    else:
        k_ref, vt_ref, kn_ref = out_refs
    tm = x_ref.shape[0]
    h = (_rms(x_ref[...]) * g_ref[...]) * (1.0 + sc_ref[...]) + sh_ref[...]
    hb = h.astype(BF16)
    lane = lax.broadcasted_iota(jnp.int32, (tm, LANES), 1)
    even = (lane % 2) == 0
    cos2 = cos_ref[...]
    sin2 = sin_ref[...]

    if full:
        qa = _dot(hb, w_ref[:, 0:Q_DIM])
        qn = jnp.zeros((tm, LANES), F32)
        scale = HEAD_DIM ** -0.5
        for hd in range(N_HEADS):
            t = _head_norm(qa[:, hd * HEAD_DIM:(hd + 1) * HEAD_DIM], qg_ref[...])
            tb = (_rope(t, cos2, sin2, even) * scale).astype(BF16)
            q_ref[:, hd * HEAD_DIM:(hd + 1) * HEAD_DIM] = tb
            qn = jnp.where(lane == hd, _l2(tb), qn)
        qn_ref[...] = qn

    ka = _dot(hb, w_ref[:, Q_DIM:Q_DIM + KV_DIM])
    kn = jnp.zeros((tm, LANES), F32)
    for g in range(N_KV_HEADS):
        t = _head_norm(ka[:, g * HEAD_DIM:(g + 1) * HEAD_DIM], kg_ref[...])
        if full:
            t = _rope(t, cos2, sin2, even)
        tb = t.astype(BF16)
        k_ref[:, g * HEAD_DIM:(g + 1) * HEAD_DIM] = tb
        kn = jnp.where(lane == g, _l2(tb), kn)
    kn_ref[...] = kn

    va = _dot(hb, w_ref[:, Q_DIM + KV_DIM:U_OFF])
    vt_ref[...] = va.T.astype(BF16)

    if full:
        ua = _dot(hb, w_ref[:, U_OFF:U_OFF + CONV_DIM])
        ub = _dot(hb, w_ref[:, U_OFF + CONV_DIM:G_OFF])
        h_ref[...] = ua * jax.nn.sigmoid(ub)
        sg_ref[...] = jax.nn.sigmoid(_dot(hb, w_ref[:, G_OFF:IN_DIM]))


def _in_proj(x, g, sc, sh, w_in_b, qg, kg, cos2, sin2, *, full, tm):
    n = x.shape[0]
    row = lambda i: (i, 0)
    col = lambda i: (0, i)
    vec = _const_spec((1, D_MODEL))
    hvec = _const_spec((1, HEAD_DIM))
    in_specs = [pl.BlockSpec((tm, D_MODEL), row), vec, vec, vec,
                _const_spec((D_MODEL, IN_DIM)), hvec, hvec,
                pl.BlockSpec((tm, LANES), row), pl.BlockSpec((tm, LANES), row)]
    k_out = (jax.ShapeDtypeStruct((n, KV_DIM), BF16), pl.BlockSpec((tm, KV_DIM), row))
    vt_out = (jax.ShapeDtypeStruct((KV_DIM, n), BF16), pl.BlockSpec((KV_DIM, tm), col))
    kn_out = (jax.ShapeDtypeStruct((n, LANES), F32), pl.BlockSpec((tm, LANES), row))
    if full:
        outs = [(jax.ShapeDtypeStruct((n, Q_DIM), BF16), pl.BlockSpec((tm, Q_DIM), row)),
                k_out, vt_out,
                (jax.ShapeDtypeStruct((n, CONV_DIM), F32), pl.BlockSpec((tm, CONV_DIM), row)),
                (jax.ShapeDtypeStruct((n, 2 * D_MODEL), F32), pl.BlockSpec((tm, 2 * D_MODEL), row)),
                (jax.ShapeDtypeStruct((n, LANES), F32), pl.BlockSpec((tm, LANES), row)),
                kn_out]
    else:
        outs = [k_out, vt_out, kn_out]
    return pl.pallas_call(
        functools.partial(_inproj_kernel, full=full),
        out_shape=[o[0] for o in outs],
        grid=(n // tm,),
        in_specs=in_specs,
        out_specs=[o[1] for o in outs],
        compiler_params=_params(("arbitrary",)),
        name="in_proj" if full else "ctx_proj",
    )(x, g, sc, sh, w_in_b, qg, kg, cos2, sin2)


def _attn_kernel(q_ref, m_ref, k_ref, vt_ref, o_ref, acc_ref):
    hd = pl.program_id(0)
    lane = lax.broadcasted_iota(jnp.int32, (TQ, LANES), 1)
    shift = jnp.sum(jnp.where(lane == hd, m_ref[...], 0.0), axis=-1, keepdims=True)
    shift_col = jnp.where(lane == 0, -shift, 0.0).astype(BF16)
    qp = jnp.concatenate([q_ref[...], shift_col], axis=1)
    acc_ref[...] = jnp.zeros_like(acc_ref)

    def body(c, carry):
        kc = k_ref[pl.ds(pl.multiple_of(c * TK, TK), TK), :]
        st = lax.dot_general(kc, qp, (((1,), (1,)), ((), ())), preferred_element_type=F32)
        pt = jnp.exp(st).astype(BF16)
        acc_ref[...] += _dot(vt_ref[c], pt)
        return carry

    lax.fori_loop(0, N_KCHUNK, body, 0)
    acc = acc_ref[...]
    out_t = acc[0:HEAD_DIM, :] / acc[HEAD_DIM:HEAD_DIM + 1, :]
    o_ref[...] = out_t.T.astype(BF16)


def _attention(q, shifts, kp, vtx):
    return pl.pallas_call(
        _attn_kernel,
        out_shape=jax.ShapeDtypeStruct((SEQ, Q_DIM), BF16),
        grid=(N_HEADS, SEQ // TQ),
        in_specs=[pl.BlockSpec((TQ, HEAD_DIM), lambda h, i: (i, h)),
                  pl.BlockSpec((TQ, LANES), lambda h, i: (i, 0)),
                  pl.BlockSpec((None, N_KEYS, 2 * HEAD_DIM), lambda h, i: (h // GROUP, 0, 0)),
                  pl.BlockSpec((None, N_KCHUNK, V_ROWS, TK), lambda h, i: (h // GROUP, 0, 0, 0))],
        out_specs=pl.BlockSpec((TQ, HEAD_DIM), lambda h, i: (i, h)),
        scratch_shapes=[pltpu.VMEM((V_ROWS, TQ), F32)],
        compiler_params=_params(("arbitrary", "arbitrary")),
        name="attention",
    )(q, shifts, kp, vtx)


def _mix_kernel(attn_ref, hm_ref, hp_ref, hn_ref, sg_ref, x_ref, g1_ref,
                wa_ref, wc_ref, wo_ref, dww_ref, dwb_ref, lng_ref, lnb_ref,
                o_ref, hbuf_ref):
    i = pl.program_id(0)
    last = pl.num_programs(0) - 1
    tm = x_ref.shape[0]
    hbuf_ref[0:CONV_HALO, :] = jnp.where(i > 0, hp_ref[...], 0.0)
    hbuf_ref[CONV_HALO:CONV_HALO + tm, :] = hm_ref[...]
    hbuf_ref[CONV_HALO + tm:, :] = jnp.where(i < last, hn_ref[...], 0.0)

    conv = jnp.zeros((tm, CONV_DIM), F32) + dwb_ref[...]
    first = CONV_HALO - CONV_WIDTH // 2
    for k in range(CONV_WIDTH):
        conv = conv + hbuf_ref[first + k:first + k + tm, :] * dww_ref[k:k + 1, :]
    mu = jnp.mean(conv, axis=-1, keepdims=True)
    cen = conv - mu
    var = jnp.mean(cen * cen, axis=-1, keepdims=True)
    ln = cen * lax.rsqrt(var + EPS) * lng_ref[...] + lnb_ref[...]
    act = ln * jax.nn.sigmoid(ln)
    y_b = _dot(act.astype(BF16), wc_ref[...])
    y_a = _dot(attn_ref[...], wa_ref[...])
    merged = sg_ref[:, 0:D_MODEL] * y_a + sg_ref[:, D_MODEL:] * y_b
    o_ref[...] = x_ref[...] + g1_ref[...] * _dot(merged.astype(BF16), wo_ref[...])


def _mix(attn, hglu, sg, x, g1, wa, wc, wo, dww, dwb, lng, lnb):
    tm = TM_MIX
    nblk = tm // CONV_HALO
    n_halo_blocks = SEQ // CONV_HALO
    row = lambda i: (i, 0)
    prev = lambda i: (jnp.maximum(i * nblk - 1, 0), 0)
    nxt = lambda i: (jnp.minimum((i + 1) * nblk, n_halo_blocks - 1), 0)
    vec = _const_spec((1, D_MODEL))
    cvec = _const_spec((1, CONV_DIM))
    return pl.pallas_call(
        _mix_kernel,
        out_shape=jax.ShapeDtypeStruct((SEQ, D_MODEL), F32),
        grid=(SEQ // tm,),
        in_specs=[pl.BlockSpec((tm, Q_DIM), row),
                  pl.BlockSpec((tm, CONV_DIM), row),
                  pl.BlockSpec((CONV_HALO, CONV_DIM), prev),
                  pl.BlockSpec((CONV_HALO, CONV_DIM), nxt),
                  pl.BlockSpec((tm, 2 * D_MODEL), row),
                  pl.BlockSpec((tm, D_MODEL), row),
                  vec,
                  _const_spec((Q_DIM, D_MODEL)), _const_spec((CONV_DIM, D_MODEL)),
                  _const_spec((D_MODEL, D_MODEL)),
                  _const_spec((CONV_WIDTH, CONV_DIM)), cvec, cvec, cvec],
        out_specs=pl.BlockSpec((tm, D_MODEL), row),
        scratch_shapes=[pltpu.VMEM((tm + 2 * CONV_HALO, CONV_DIM), F32)],
        compiler_params=_params(("arbitrary",)),
        name="mix",
    )(attn, hglu, hglu, hglu, sg, x, g1, wa, wc, wo, dww, dwb, lng, lnb)


def _ffn_kernel(xm_ref, xp_ref, xn_ref, ng_ref, sc_ref, sh_ref, g2_ref, fg_ref,
                wup_ref, dw_ref, wdn_ref, o_ref, xbuf_ref, hb_ref, abuf_ref, acc_ref):
    i = pl.program_id(0)
    tm = xm_ref.shape[0]
    ext = tm + 2 * FFN_HALO
    xbuf_ref[0:FFN_HALO, :] = xp_ref[...]
    xbuf_ref[FFN_HALO:FFN_HALO + tm, :] = xm_ref[...]
    xbuf_ref[FFN_HALO + tm:, :] = xn_ref[...]
    h = (_rms(xbuf_ref[...]) * ng_ref[...]) * (1.0 + sc_ref[...]) + sh_ref[...]
    hb_ref[...] = h.astype(BF16)
    grow = lax.broadcasted_iota(jnp.int32, (ext, 1), 0) + (i * tm - FFN_HALO)
    valid = (grow >= 0) & (grow < SEQ)
    acc_ref[...] = jnp.zeros_like(acc_ref)

    def body(c, carry):
        w = wup_ref[c]
        a = _dot(hb_ref[...], w[:, 0:FFN_CHUNK])
        abuf_ref[...] = jnp.where(valid, a, 0.0)
        b = _dot(hb_ref[FFN_HALO:FFN_HALO + tm, :], w[:, FFN_CHUNK:])
        dw = dw_ref[c]
        conv = (abuf_ref[FFN_HALO - 1:FFN_HALO - 1 + tm, :] * dw[0:1, :]
                + abuf_ref[FFN_HALO:FFN_HALO + tm, :] * dw[1:2, :]
                + abuf_ref[FFN_HALO + 1:FFN_HALO + 1 + tm, :] * dw[2:3, :]
                + dw[3:4, :])
        act = jax.nn.gelu(conv, approximate=True) * b
        acc_ref[...] += _dot(act.astype(BF16), wdn_ref[c])
        return carry

    lax.fori_loop(0, N_FCHUNK, body, 0)
    x2 = xm_ref[...] + g2_ref[...] * acc_ref[...]
    o_ref[...] = _rms(x2) * fg_ref[...]


def _ffn(x1, ng, sc, sh, g2, fg, wup_c, dw_c, wdn_c):
    tm = TM_FFN
    nblk = tm // FFN_HALO
    n_halo_blocks = SEQ // FFN_HALO
    row = lambda i: (i, 0)
    prev = lambda i: (jnp.maximum(i * nblk - 1, 0), 0)
    nxt = lambda i: (jnp.minimum((i + 1) * nblk, n_halo_blocks - 1), 0)
    vec = _const_spec((1, D_MODEL))
    ext = tm + 2 * FFN_HALO
    return pl.pallas_call(
        _ffn_kernel,
        out_shape=jax.ShapeDtypeStruct((SEQ, D_MODEL), F32),
        grid=(SEQ // tm,),
        in_specs=[pl.BlockSpec((tm, D_MODEL), row),
                  pl.BlockSpec((FFN_HALO, D_MODEL), prev),
                  pl.BlockSpec((FFN_HALO, D_MODEL), nxt),
                  vec, vec, vec, vec, vec,
                  _const_spec((N_FCHUNK, D_MODEL, 2 * FFN_CHUNK)),
                  _const_spec((N_FCHUNK, SUBLANES_F32, FFN_CHUNK)),
                  _const_spec((N_FCHUNK, FFN_CHUNK, D_MODEL))],
        out_specs=pl.BlockSpec((tm, D_MODEL), row),
        scratch_shapes=[pltpu.VMEM((ext, D_MODEL), F32),
                        pltpu.VMEM((ext, D_MODEL), BF16),
                        pltpu.VMEM((ext, FFN_CHUNK), F32),
                        pltpu.VMEM((tm, D_MODEL), F32)],
        compiler_params=_params(("arbitrary",)),
        name="ffn",
    )(x1, x1, x1, ng, sc, sh, g2, fg, wup_c, dw_c, wdn_c)


def _rope_tables():
    half = HEAD_DIM // 2
    rows = SEQ // GRID_W
    inv_freq = ROPE_THETA ** (-jnp.arange(0, half, 2, dtype=F32) / half)
    r = jnp.repeat(jnp.arange(rows, dtype=F32), GRID_W)
    col = jnp.tile(jnp.arange(GRID_W, dtype=F32), rows)
    ang = jnp.concatenate([r[:, None] * inv_freq, col[:, None] * inv_freq], axis=-1)
    cos, sin = jnp.cos(ang), jnp.sin(ang)
    cos2 = jnp.repeat(cos, 2, axis=-1)
    sin2 = jnp.stack([-sin, sin], axis=-1).reshape(SEQ, HEAD_DIM)
    return cos2, sin2


def kernel(x, c, ctx, c_ctx, w_mod, b_mod, norm1_g, w_in, q_norm_g, k_norm_g, w_attn_out,
           conv_dw_w, conv_dw_b, conv_ln_g, conv_ln_b, w_conv_out, w_out, norm2_g,
           w_up, ffn_dw_w, ffn_dw_b, w_down, final_g):
    assert x.shape == (1, SEQ, D_MODEL) and ctx.shape == (1, CTX_LEN, D_MODEL)
    assert w_mod.shape[0] == 1, "single layer"
    x2d = x[0]
    ctx2d = ctx[0]
    r1 = lambda v: v.reshape(1, -1)

    cvecs = jnp.zeros((SUBLANES_F32, D_MODEL), F32).at[0].set(c[0]).at[1].set(c_ctx)
    mod = _modulation(cvecs, w_mod[0], r1(b_mod[0]))
    sh1, sc1, g1, sh2, sc2, g2 = [r1(mod[0, j * D_MODEL:(j + 1) * D_MODEL]) for j in range(6)]
    csh1, csc1 = [r1(mod[1, j * D_MODEL:(j + 1) * D_MODEL]) for j in range(2)]

    w_in_b = w_in[0].astype(BF16)
    cos2, sin2 = _rope_tables()
    ng1, qg, kg = r1(norm1_g[0]), r1(q_norm_g[0]), r1(k_norm_g[0])
    q, kx, vtx, hglu, sg, qn, knx = _in_proj(x2d, ng1, sc1, sh1, w_in_b, qg, kg, cos2, sin2,
                                            full=True, tm=TM_PROJ)
    kc, vtc, knc = _in_proj(ctx2d, ng1, csc1, csh1, w_in_b, qg, kg,
                            cos2[:CTX_LEN], sin2[:CTX_LEN], full=False, tm=CTX_LEN)

    k_all = jnp.concatenate([kx, kc], axis=0).reshape(N_KEYS, N_KV_HEADS, HEAD_DIM)
    one_col = jnp.zeros((N_KEYS, N_KV_HEADS, HEAD_DIM), BF16).at[:, :, 0].set(1.0)
    kp = jnp.concatenate([k_all, one_col], axis=-1).transpose(1, 0, 2)
    vt_all = jnp.concatenate([vtx, vtc], axis=1).reshape(N_KV_HEADS, HEAD_DIM, N_KEYS)
    ones_rows = jnp.ones((N_KV_HEADS, SUBLANES_BF16, N_KEYS), BF16)
    vt_ext = jnp.concatenate([vt_all, ones_rows], axis=1)
    vt_ext = vt_ext.reshape(N_KV_HEADS, V_ROWS, N_KCHUNK, TK).transpose(0, 2, 1, 3)
    kmax = jnp.maximum(jnp.max(knx, axis=0), jnp.max(knc, axis=0))
    kmax_per_head = jnp.zeros((LANES,), F32).at[:N_HEADS].set(jnp.repeat(kmax[:N_KV_HEADS], GROUP))
    shifts = qn * kmax_per_head[None, :]
    attn = _attention(q, shifts, kp, vt_ext)

    x1 = _mix(attn, hglu, sg, x2d, g1,
              w_attn_out[0].astype(BF16), w_conv_out[0].astype(BF16), w_out[0].astype(BF16),
              conv_dw_w[0], r1(conv_dw_b[0]), r1(conv_ln_g[0]), r1(conv_ln_b[0]))

    wu = w_up[0].astype(BF16)
    wup_c = jnp.concatenate([wu[:, :FFN_DIM].reshape(D_MODEL, N_FCHUNK, FFN_CHUNK),
                             wu[:, FFN_DIM:].reshape(D_MODEL, N_FCHUNK, FFN_CHUNK)],
                            axis=-1).transpose(1, 0, 2)
    dw_rows = jnp.concatenate([ffn_dw_w[0], ffn_dw_b[0][None, :],
                               jnp.zeros((SUBLANES_F32 - 4, FFN_DIM), F32)], axis=0)
    dw_c = dw_rows.reshape(SUBLANES_F32, N_FCHUNK, FFN_CHUNK).transpose(1, 0, 2)
    wdn_c = w_down[0].astype(BF16).reshape(N_FCHUNK, FFN_CHUNK, D_MODEL)
    out = _ffn(x1, r1(norm2_g[0]), sc2, sh2, g2, r1(final_g), wup_c, dw_c, wdn_c)
    return out[None]
```

```python
import functools

import jax
import jax.numpy as jnp
from jax import lax
from jax.experimental import pallas as pl
from jax.experimental.pallas import tpu as pltpu

D_MODEL = 1024
SEQ = 16384
GRID_W = 64
CTX_LEN = 256
N_HEADS = 8
N_KV_HEADS = 2
HEAD_DIM = 128
ROPE_THETA = 10000.0
CONV_DIM = 512
CONV_WIDTH = 31
FFN_DIM = 2816
EPS = 1e-6
Q_DIM = N_HEADS * HEAD_DIM
KV_DIM = N_KV_HEADS * HEAD_DIM
GROUP = N_HEADS // N_KV_HEADS
U_OFF = Q_DIM + 2 * KV_DIM
G_OFF = U_OFF + 2 * CONV_DIM
IN_DIM = G_OFF + 2 * D_MODEL

F32 = jnp.float32
BF16 = jnp.bfloat16

LANES = 128
SUBLANES_F32 = 8
MXU_DIM = 256
VMEM_LIMIT_BYTES = 56 * 1024 * 1024

TM_PROJ = 512
TQ = 1024
TK = 1024
KP_DIM = 2 * HEAD_DIM
TM_MIX = 512
CONV_HALO = 16
TM_FFN = 512
FFN_HALO = SUBLANES_F32
FFN_CHUNK = MXU_DIM
SAFE_SHIFT = 40.0

assert SEQ % TK == 0 and SEQ % TQ == 0 and SEQ % TM_PROJ == 0 and CTX_LEN % MXU_DIM == 0
assert FFN_DIM % FFN_CHUNK == 0 and CONV_HALO >= CONV_WIDTH // 2


def _dot(a, b):
    return jnp.dot(a, b, preferred_element_type=F32)


def _dot_nt(a, b):
    return lax.dot_general(a, b, (((1,), (1,)), ((), ())), preferred_element_type=F32)


def _rms(x):
    return x * lax.rsqrt(jnp.mean(x * x, axis=-1, keepdims=True) + EPS)


def _const_spec(shape):
    return pl.BlockSpec(shape, lambda *_: (0,) * len(shape), pipeline_mode=pl.Buffered(1))


def _params(sem):
    return pltpu.CompilerParams(dimension_semantics=sem, vmem_limit_bytes=VMEM_LIMIT_BYTES)


def _mod_kernel(c_ref, w_ref, b_ref, o_ref):
    c = c_ref[...]
    s = c * jax.nn.sigmoid(c)
    w = w_ref[...]
    s_hi = s.astype(BF16)
    s_lo = (s - s_hi.astype(F32)).astype(BF16)
    w_hi = w.astype(BF16)
    w_lo = (w - w_hi.astype(F32)).astype(BF16)
    o_ref[...] = _dot(s_hi, w_hi) + _dot(s_hi, w_lo) + _dot(s_lo, w_hi) + b_ref[...]


def _modulation(cvecs, w_mod, b_mod):
    rows, n_out = cvecs.shape[0], w_mod.shape[1]
    bn = 1024
    return pl.pallas_call(
        _mod_kernel,
        out_shape=jax.ShapeDtypeStruct((rows, n_out), F32),
        grid=(n_out // bn,),
        in_specs=[pl.BlockSpec((rows, D_MODEL), lambda j: (0, 0)),
                  pl.BlockSpec((D_MODEL, bn), lambda j: (0, j)),
                  pl.BlockSpec((1, bn), lambda j: (0, j))],
        out_specs=pl.BlockSpec((rows, bn), lambda j: (0, j)),
        compiler_params=_params(("arbitrary",)),
        name="modulation",
    )(cvecs, w_mod, b_mod)


def _rope(t, cos2, sin2, even):
    swapped = jnp.where(even, pltpu.roll(t, HEAD_DIM - 1, 1), pltpu.roll(t, 1, 1))
    return t * cos2 + swapped * sin2


def _l2(tb):
    t = tb.astype(F32)
    return jnp.sqrt(jnp.sum(t * t, axis=-1, keepdims=True))


def _inproj_kernel(x_ref, g_ref, sc_ref, sh_ref, w_ref, qg_ref, kg_ref, cos_ref, sin_ref,
                   *out_refs, full):
    if full:
        q_ref, k_ref, vt_ref, h_ref, sg_ref, qn_ref, kn_ref = out_refs
    else:
        k_ref, vt_ref, kn_ref = out_refs
    tm = x_ref.shape[0]
    h = (_rms(x_ref[...]) * g_ref[...]) * (1.0 + sc_ref[...]) + sh_ref[...]
    hb = h.astype(BF16)
    lane = lax.broadcasted_iota(jnp.int32, (tm, LANES), 1)
    even = (lane % 2) == 0

    if full:
        cos2 = cos_ref[...]
        sin2 = sin_ref[...]
        qa = _dot(hb, w_ref[:, 0:Q_DIM])
        qn = jnp.zeros((tm, LANES), F32)
        scale = HEAD_DIM ** -0.5
        for hd in range(N_HEADS):
            t = _rms(qa[:, hd * HEAD_DIM:(hd + 1) * HEAD_DIM]) * qg_ref[...]
            tb = (_rope(t, cos2, sin2, even) * scale).astype(BF16)
            q_ref[:, hd * HEAD_DIM:(hd + 1) * HEAD_DIM] = tb
            qn = jnp.where(lane == hd, _l2(tb), qn)
        qn_ref[...] = qn

    ka = _dot(hb, w_ref[:, Q_DIM:Q_DIM + KV_DIM])
    kn = jnp.zeros((tm, LANES), F32)
    ones_col = jnp.where(lane == 0, 1.0, 0.0).astype(BF16)
    for g in range(N_KV_HEADS):
        t = _rms(ka[:, g * HEAD_DIM:(g + 1) * HEAD_DIM]) * kg_ref[...]
        if full:
            t = _rope(t, cos2, sin2, even)
        tb = t.astype(BF16)
        k_ref[g, :, 0:HEAD_DIM] = tb
        k_ref[g, :, HEAD_DIM:] = ones_col
        kn = jnp.where(lane == g, _l2(tb), kn)
    kn_ref[...] = jnp.broadcast_to(jnp.max(kn, axis=0, keepdims=True), (SUBLANES_F32, LANES))

    va = _dot(hb, w_ref[:, Q_DIM + KV_DIM:U_OFF])
    vt_ref[...] = va.T.astype(BF16)

    if full:
        ua = _dot(hb, w_ref[:, U_OFF:U_OFF + CONV_DIM])
        ub = _dot(hb, w_ref[:, U_OFF + CONV_DIM:G_OFF])
        h_ref[...] = ua * jax.nn.sigmoid(ub)
        sg_ref[...] = jax.nn.sigmoid(_dot(hb, w_ref[:, G_OFF:IN_DIM]))


def _in_proj(x, g, sc, sh, w_in_b, qg, kg, cos2, sin2, *, full, tm):
    n = x.shape[0]
    row = lambda i: (i, 0)
    vec = _const_spec((1, D_MODEL))
    hvec = _const_spec((1, HEAD_DIM))
    in_specs = [pl.BlockSpec((tm, D_MODEL), row), vec, vec, vec,
                _const_spec((D_MODEL, IN_DIM)), hvec, hvec,
                pl.BlockSpec((tm, LANES), row), pl.BlockSpec((tm, LANES), row)]
    k_out = (jax.ShapeDtypeStruct((N_KV_HEADS, n, KP_DIM), BF16),
             pl.BlockSpec((N_KV_HEADS, tm, KP_DIM), lambda i: (0, i, 0)))
    vt_out = (jax.ShapeDtypeStruct((KV_DIM, n), BF16), pl.BlockSpec((KV_DIM, tm), lambda i: (0, i)))
    kn_out = (jax.ShapeDtypeStruct((n // tm * SUBLANES_F32, LANES), F32),
              pl.BlockSpec((SUBLANES_F32, LANES), row))
    if full:
        outs = [(jax.ShapeDtypeStruct((n, Q_DIM), BF16), pl.BlockSpec((tm, Q_DIM), row)),
                k_out, vt_out,
                (jax.ShapeDtypeStruct((n, CONV_DIM), F32), pl.BlockSpec((tm, CONV_DIM), row)),
                (jax.ShapeDtypeStruct((n, 2 * D_MODEL), F32), pl.BlockSpec((tm, 2 * D_MODEL), row)),
                (jax.ShapeDtypeStruct((n, LANES), F32), pl.BlockSpec((tm, LANES), row)),
                kn_out]
    else:
        outs = [k_out, vt_out, kn_out]
    return pl.pallas_call(
        functools.partial(_inproj_kernel, full=full),
        out_shape=[o[0] for o in outs],
        grid=(n // tm,),
        in_specs=in_specs,
        out_specs=[o[1] for o in outs],
        compiler_params=_params(("arbitrary",)),
        name="in_proj" if full else "ctx_proj",
    )(x, g, sc, sh, w_in_b, qg, kg, cos2, sin2)


def _head_column(ref, hd):
    lane = lax.broadcasted_iota(jnp.int32, ref.shape, 1)
    return jnp.sum(jnp.where(lane == hd, ref[...], 0.0), axis=-1, keepdims=True)


def _shifted_queries(q_ref, shift):
    lane = lax.broadcasted_iota(jnp.int32, (q_ref.shape[0], LANES), 1)
    shift_col = jnp.where(lane == 0, -shift, 0.0).astype(BF16)
    return jnp.concatenate([q_ref[...], shift_col], axis=1)


def _attn_kernel(q_ref, m_ref, ms_ref, kx_ref, vtx_ref, kc_ref, vtc_ref, o_ref):
    hd = pl.program_id(0)
    shift = _head_column(m_ref, hd) * jnp.sum(ms_ref[...], axis=-1, keepdims=True)
    qp = _shifted_queries(q_ref, shift)

    def chunk(k_blk, vt_blk):
        p = jnp.exp(_dot_nt(k_blk, qp))
        psum = jnp.sum(p.reshape(-1, SUBLANES_F32, TQ), axis=0)
        return _dot(vt_blk, p.astype(BF16)), psum

    acc, psum = chunk(kc_ref[...], vtc_ref[...])
    for c in range(SEQ // TK):
        a, s = chunk(kx_ref[c * TK:(c + 1) * TK, :], vtx_ref[:, c * TK:(c + 1) * TK])
        acc = acc + a
        psum = psum + s
    out_t = acc / jnp.sum(psum, axis=0, keepdims=True)
    o_ref[...] = out_t.T.astype(BF16)


def _kv_specs():
    kv = lambda h, i: (h // GROUP, 0, 0)
    one = pl.Buffered(1)
    return [pl.BlockSpec((None, SEQ, KP_DIM), kv, pipeline_mode=one),
            pl.BlockSpec((None, HEAD_DIM, SEQ), kv, pipeline_mode=one),
            pl.BlockSpec((None, CTX_LEN, KP_DIM), kv, pipeline_mode=one),
            pl.BlockSpec((None, HEAD_DIM, CTX_LEN), kv, pipeline_mode=one)]


def _attention(q, m, mscale, kx, vtx, kc, vtc):
    return pl.pallas_call(
        _attn_kernel,
        out_shape=jax.ShapeDtypeStruct((SEQ, Q_DIM), BF16),
        grid=(N_HEADS, SEQ // TQ),
        in_specs=[pl.BlockSpec((TQ, HEAD_DIM), lambda h, i: (i, h)),
                  pl.BlockSpec((TQ, LANES), lambda h, i: (i, 0)),
                  pl.BlockSpec((None, 1, LANES), lambda h, i: (h, 0, 0))] + _kv_specs(),
        out_specs=pl.BlockSpec((TQ, HEAD_DIM), lambda h, i: (i, h)),
        compiler_params=_params(("arbitrary", "arbitrary")),
        name="attention",
    )(q, m, mscale, kx, vtx, kc, vtc)


def _rowmax_kernel(q_ref, kx_ref, kc_ref, o_ref):
    hd = pl.program_id(0)
    qp = _shifted_queries(q_ref, jnp.zeros((q_ref.shape[0], 1), F32))
    m = jnp.max(_dot_nt(qp, kc_ref[...]), axis=-1, keepdims=True)

    def body(c, m):
        k_blk = kx_ref[pl.ds(pl.multiple_of(c * TK, TK), TK), :]
        return jnp.maximum(m, jnp.max(_dot_nt(qp, k_blk), axis=-1, keepdims=True))

    m = lax.fori_loop(0, SEQ // TK, body, m)
    lane = lax.broadcasted_iota(jnp.int32, o_ref.shape[1:], 1)
    o_ref[0] = jnp.where(lane == hd, m, 0.0)


def _row_max(q, kx, kc):
    tq = 256
    kv = lambda h, i: (h // GROUP, 0, 0)
    per_head = pl.pallas_call(
        _rowmax_kernel,
        out_shape=jax.ShapeDtypeStruct((N_HEADS, SEQ, LANES), F32),
        grid=(N_HEADS, SEQ // tq),
        in_specs=[pl.BlockSpec((tq, HEAD_DIM), lambda h, i: (i, h)),
                  pl.BlockSpec((None, SEQ, KP_DIM), kv, pipeline_mode=pl.Buffered(1)),
                  pl.BlockSpec((None, CTX_LEN, KP_DIM), kv, pipeline_mode=pl.Buffered(1))],
        out_specs=pl.BlockSpec((1, tq, LANES), lambda h, i: (h, i, 0)),
        compiler_params=_params(("arbitrary", "arbitrary")),
        name="row_max",
    )(q, kx, kc)
    return jnp.sum(per_head, axis=0)


def _mix_kernel(attn_ref, hm_ref, hp_ref, hn_ref, sg_ref, x_ref, g1_ref,
                wa_ref, wc_ref, wo_ref, dww_ref, dwb_ref, lng_ref, lnb_ref,
                o_ref, hbuf_ref, sbuf_ref):
    i = pl.program_id(0)
    last = pl.num_programs(0) - 1
    tm = x_ref.shape[0]
    hbuf_ref[0:CONV_HALO, :] = jnp.where(i > 0, hp_ref[...], 0.0)
    hbuf_ref[CONV_HALO:CONV_HALO + tm, :] = hm_ref[...]
    hbuf_ref[CONV_HALO + tm:, :] = jnp.where(i < last, hn_ref[...], 0.0)
    span = tm + 2 * CONV_HALO - SUBLANES_F32
    for r in range(1, SUBLANES_F32):
        sbuf_ref[r - 1, 0:span, :] = hbuf_ref[r:r + span, :]

    first = CONV_HALO - CONV_WIDTH // 2
    conv = None
    for k in range(CONV_WIDTH):
        a, r = divmod(first + k, SUBLANES_F32)
        src = hbuf_ref if r == 0 else sbuf_ref.at[r - 1]
        term = src[a * SUBLANES_F32:a * SUBLANES_F32 + tm, :] * dww_ref[k:k + 1, :]
        conv = term + dwb_ref[...] if conv is None else conv + term
    mu = jnp.mean(conv, axis=-1, keepdims=True)
    cen = conv - mu
    var = jnp.mean(cen * cen, axis=-1, keepdims=True)
    ln = cen * lax.rsqrt(var + EPS) * lng_ref[...] + lnb_ref[...]
    act = ln * jax.nn.sigmoid(ln)
    y_b = _dot(act.astype(BF16), wc_ref[...])
    y_a = _dot(attn_ref[...], wa_ref[...])
    merged = sg_ref[:, 0:D_MODEL] * y_a + sg_ref[:, D_MODEL:] * y_b
    o_ref[...] = x_ref[...] + g1_ref[...] * _dot(merged.astype(BF16), wo_ref[...])


def _halo_maps(tm, halo):
    nblk = tm // halo
    n_halo_blocks = SEQ // halo
    prev = lambda i: (jnp.maximum(i * nblk - 1, 0), 0)
    nxt = lambda i: (jnp.minimum((i + 1) * nblk, n_halo_blocks - 1), 0)
    return prev, nxt


def _mix(attn, hglu, sg, x, g1, wa, wc, wo, dww, dwb, lng, lnb):
    tm = TM_MIX
    row = lambda i: (i, 0)
    prev, nxt = _halo_maps(tm, CONV_HALO)
    cvec = _const_spec((1, CONV_DIM))
    ext = tm + 2 * CONV_HALO
    return pl.pallas_call(
        _mix_kernel,
        out_shape=jax.ShapeDtypeStruct((SEQ, D_MODEL), F32),
        grid=(SEQ // tm,),
        in_specs=[pl.BlockSpec((tm, Q_DIM), row),
                  pl.BlockSpec((tm, CONV_DIM), row),
                  pl.BlockSpec((CONV_HALO, CONV_DIM), prev),
                  pl.BlockSpec((CONV_HALO, CONV_DIM), nxt),
                  pl.BlockSpec((tm, 2 * D_MODEL), row),
                  pl.BlockSpec((tm, D_MODEL), row),
                  _const_spec((1, D_MODEL)),
                  _const_spec((Q_DIM, D_MODEL)), _const_spec((CONV_DIM, D_MODEL)),
                  _const_spec((D_MODEL, D_MODEL)),
                  _const_spec((CONV_WIDTH, CONV_DIM)), cvec, cvec, cvec],
        out_specs=pl.BlockSpec((tm, D_MODEL), row),
        scratch_shapes=[pltpu.VMEM((ext, CONV_DIM), F32),
                        pltpu.VMEM((SUBLANES_F32 - 1, ext, CONV_DIM), F32)],
        compiler_params=_params(("arbitrary",)),
        name="mix",
    )(attn, hglu, hglu, hglu, sg, x, g1, wa, wc, wo, dww, dwb, lng, lnb)


def _ffn_kernel(xm_ref, xp_ref, xn_ref, ng_ref, sc_ref, sh_ref, g2_ref, fg_ref,
                wup_ref, dw_ref, wdn_ref, o_ref, xbuf_ref, abuf_ref, act_ref):
    i = pl.program_id(0)
    tm = xm_ref.shape[0]
    ext = tm + 2 * FFN_HALO
    xbuf_ref[0:FFN_HALO, :] = xp_ref[...]
    xbuf_ref[FFN_HALO:FFN_HALO + tm, :] = xm_ref[...]
    xbuf_ref[FFN_HALO + tm:, :] = xn_ref[...]
    h = (_rms(xbuf_ref[...]) * ng_ref[...]) * (1.0 + sc_ref[...]) + sh_ref[...]
    hb = h.astype(BF16)
    grow = lax.broadcasted_iota(jnp.int32, (ext, 1), 0) + (i * tm - FFN_HALO)
    valid = (grow >= 0) & (grow < SEQ)
    for off in range(0, FFN_DIM, FFN_CHUNK):
        cols = slice(off, off + FFN_CHUNK)
        a = _dot(hb, wup_ref[:, cols])
        abuf_ref[:, cols] = jnp.where(valid, a, 0.0)
        b = _dot(hb[FFN_HALO:FFN_HALO + tm, :], wup_ref[:, FFN_DIM + off:FFN_DIM + off + FFN_CHUNK])
        conv = (abuf_ref[FFN_HALO - 1:FFN_HALO - 1 + tm, cols] * dw_ref[0:1, cols]
                + abuf_ref[FFN_HALO:FFN_HALO + tm, cols] * dw_ref[1:2, cols]
                + abuf_ref[FFN_HALO + 1:FFN_HALO + 1 + tm, cols] * dw_ref[2:3, cols]
                + dw_ref[3:4, cols])
        act_ref[:, cols] = (jax.nn.gelu(conv, approximate=True) * b).astype(BF16)
    x2 = xm_ref[...] + g2_ref[...] * _dot(act_ref[...], wdn_ref[...])
    o_ref[...] = _rms(x2) * fg_ref[...]


def _ffn(x1, ng, sc, sh, g2, fg, wup, dw, wdn):
    tm = TM_FFN
    row = lambda i: (i, 0)
    prev, nxt = _halo_maps(tm, FFN_HALO)
    vec = _const_spec((1, D_MODEL))
    ext = tm + 2 * FFN_HALO
    return pl.pallas_call(
        _ffn_kernel,
        out_shape=jax.ShapeDtypeStruct((SEQ, D_MODEL), F32),
        grid=(SEQ // tm,),
        in_specs=[pl.BlockSpec((tm, D_MODEL), row),
                  pl.BlockSpec((FFN_HALO, D_MODEL), prev),
                  pl.BlockSpec((FFN_HALO, D_MODEL), nxt),
                  vec, vec, vec, vec, vec,
                  _const_spec((D_MODEL, 2 * FFN_DIM)),
                  _const_spec((SUBLANES_F32, FFN_DIM)),
                  _const_spec((FFN_DIM, D_MODEL))],
        out_specs=pl.BlockSpec((tm, D_MODEL), row),
        scratch_shapes=[pltpu.VMEM((ext, D_MODEL), F32),
                        pltpu.VMEM((ext, FFN_DIM), F32),
                        pltpu.VMEM((tm, FFN_DIM), BF16)],
        compiler_params=_params(("arbitrary",)),
        name="ffn",
    )(x1, x1, x1, ng, sc, sh, g2, fg, wup, dw, wdn)


def _rope_tables():
    half = HEAD_DIM // 2
    rows = SEQ // GRID_W
    inv_freq = ROPE_THETA ** (-jnp.arange(0, half, 2, dtype=F32) / half)
    row_ang = jnp.arange(rows, dtype=F32)[:, None] * inv_freq
    col_ang = jnp.arange(GRID_W, dtype=F32)[:, None] * inv_freq

    def pair_tables(ang):
        c, s = jnp.cos(ang), jnp.sin(ang)
        return jnp.repeat(c, 2, axis=-1), jnp.stack([-s, s], axis=-1).reshape(ang.shape[0], half)

    rc, rs = pair_tables(row_ang)
    cc, cs = pair_tables(col_ang)
    expand = lambda r, c: jnp.concatenate(
        [jnp.broadcast_to(r[:, None, :], (rows, GRID_W, half)),
         jnp.broadcast_to(c[None, :, :], (rows, GRID_W, half))], axis=-1).reshape(SEQ, HEAD_DIM)
    return expand(rc, cc), expand(rs, cs)


def kernel(x, c, ctx, c_ctx, w_mod, b_mod, norm1_g, w_in, q_norm_g, k_norm_g, w_attn_out,
           conv_dw_w, conv_dw_b, conv_ln_g, conv_ln_b, w_conv_out, w_out, norm2_g,
           w_up, ffn_dw_w, ffn_dw_b, w_down, final_g):
    assert x.shape == (1, SEQ, D_MODEL) and ctx.shape == (1, CTX_LEN, D_MODEL)
    assert w_mod.shape[0] == 1, "single layer"
    x2d = x[0]
    ctx2d = ctx[0]
    r1 = lambda v: v.reshape(1, -1)

    cvecs = jnp.zeros((SUBLANES_F32, D_MODEL), F32).at[0].set(c[0]).at[1].set(c_ctx)
    mod = _modulation(cvecs, w_mod[0], r1(b_mod[0]))
    sh1, sc1, g1, sh2, sc2, g2 = [r1(mod[0, j * D_MODEL:(j + 1) * D_MODEL]) for j in range(6)]
    csh1, csc1 = [r1(mod[1, j * D_MODEL:(j + 1) * D_MODEL]) for j in range(2)]

    w_in_b = w_in[0].astype(BF16)
    cos2, sin2 = _rope_tables()
    ng1, qg, kg = r1(norm1_g[0]), r1(q_norm_g[0]), r1(k_norm_g[0])
    q, kx, vtx, hglu, sg, qn, knx = _in_proj(x2d, ng1, sc1, sh1, w_in_b, qg, kg, cos2, sin2,
                                            full=True, tm=TM_PROJ)
    kc, vtc, knc = _in_proj(ctx2d, ng1, csc1, csh1, w_in_b, qg, kg,
                            cos2[:CTX_LEN], sin2[:CTX_LEN], full=False, tm=CTX_LEN)
    vtx = vtx.reshape(N_KV_HEADS, HEAD_DIM, SEQ)
    vtc = vtc.reshape(N_KV_HEADS, HEAD_DIM, CTX_LEN)

    kmax = jnp.maximum(jnp.max(knx, axis=0), jnp.max(knc, axis=0))[:N_KV_HEADS]
    kmax_head = jnp.repeat(kmax, GROUP)
    qmax = jnp.max(qn, axis=0)[:N_HEADS]
    lane0 = jnp.zeros((N_HEADS, 1, LANES), F32).at[:, 0, 0]
    m, mscale = lax.cond(
        jnp.max(qmax * kmax_head) > SAFE_SHIFT,
        lambda: (_row_max(q, kx, kc), lane0.set(1.0)),
        lambda: (qn, lane0.set(kmax_head)))
    attn = _attention(q, m, mscale, kx, vtx, kc, vtc)

    x1 = _mix(attn, hglu, sg, x2d, g1,
              w_attn_out[0].astype(BF16), w_conv_out[0].astype(BF16), w_out[0].astype(BF16),
              conv_dw_w[0], r1(conv_dw_b[0]), r1(conv_ln_g[0]), r1(conv_ln_b[0]))

    dw = jnp.concatenate([ffn_dw_w[0], ffn_dw_b[0][None, :],
                          jnp.zeros((SUBLANES_F32 - 4, FFN_DIM), F32)], axis=0)
    out = _ffn(x1, r1(norm2_g[0]), sc2, sh2, g2, r1(final_g),
               w_up[0].astype(BF16), dw, w_down[0].astype(BF16))
    return out[None]
```

```python
import functools

import jax
import jax.numpy as jnp
from jax import lax
from jax.experimental import pallas as pl
from jax.experimental.pallas import tpu as pltpu

D_MODEL = 1024
SEQ = 16384
GRID_W = 64
CTX_LEN = 256
N_HEADS = 8
N_KV_HEADS = 2
HEAD_DIM = 128
ROPE_THETA = 10000.0
CONV_DIM = 512
CONV_WIDTH = 31
FFN_DIM = 2816
EPS = 1e-6
Q_DIM = N_HEADS * HEAD_DIM
KV_DIM = N_KV_HEADS * HEAD_DIM
GROUP = N_HEADS // N_KV_HEADS
U_OFF = Q_DIM + 2 * KV_DIM
G_OFF = U_OFF + 2 * CONV_DIM
IN_DIM = G_OFF + 2 * D_MODEL

F32 = jnp.float32
BF16 = jnp.bfloat16

LANES = 128
SUBLANES_F32 = 8
MXU_DIM = 256
VMEM_LIMIT_BYTES = 56 * 1024 * 1024

TM_PROJ = 512
TQ = 1024
TK = 8192
KP_DIM = 2 * HEAD_DIM
TM_MIX = 512
CONV_HALO = 16
TM_FFN = 512
FFN_HALO = SUBLANES_F32
FFN_CHUNK = MXU_DIM
SAFE_SHIFT = 40.0

assert SEQ % TK == 0 and SEQ % TQ == 0 and SEQ % TM_PROJ == 0 and CTX_LEN % MXU_DIM == 0
assert FFN_DIM % FFN_CHUNK == 0 and CONV_HALO >= CONV_WIDTH // 2


def _dot(a, b):
    return jnp.dot(a, b, preferred_element_type=F32)


def _dot_nt(a, b):
    return lax.dot_general(a, b, (((1,), (1,)), ((), ())), preferred_element_type=F32)


def _rms(x):
    return x * lax.rsqrt(jnp.mean(x * x, axis=-1, keepdims=True) + EPS)


def _const_spec(shape):
    return pl.BlockSpec(shape, lambda *_: (0,) * len(shape), pipeline_mode=pl.Buffered(1))


def _params(sem):
    return pltpu.CompilerParams(dimension_semantics=sem, vmem_limit_bytes=VMEM_LIMIT_BYTES)


def _mod_kernel(c_ref, w_ref, b_ref, o_ref):
    c = c_ref[...]
    s = c * jax.nn.sigmoid(c)
    w = w_ref[...]
    s_hi = s.astype(BF16)
    s_lo = (s - s_hi.astype(F32)).astype(BF16)
    w_hi = w.astype(BF16)
    w_lo = (w - w_hi.astype(F32)).astype(BF16)
    o_ref[...] = _dot(s_hi, w_hi) + _dot(s_hi, w_lo) + _dot(s_lo, w_hi) + b_ref[...]


def _modulation(cvecs, w_mod, b_mod):
    rows, n_out = cvecs.shape[0], w_mod.shape[1]
    bn = 1024
    return pl.pallas_call(
        _mod_kernel,
        out_shape=jax.ShapeDtypeStruct((rows, n_out), F32),
        grid=(n_out // bn,),
        in_specs=[pl.BlockSpec((rows, D_MODEL), lambda j: (0, 0)),
                  pl.BlockSpec((D_MODEL, bn), lambda j: (0, j)),
                  pl.BlockSpec((1, bn), lambda j: (0, j))],
        out_specs=pl.BlockSpec((rows, bn), lambda j: (0, j)),
        compiler_params=_params(("arbitrary",)),
        name="modulation",
    )(cvecs, w_mod, b_mod)


def _rope(t, cos2, sin2, even):
    swapped = jnp.where(even, pltpu.roll(t, HEAD_DIM - 1, 1), pltpu.roll(t, 1, 1))
    return t * cos2 + swapped * sin2


def _l2(tb):
    t = tb.astype(F32)
    return jnp.sqrt(jnp.sum(t * t, axis=-1, keepdims=True))


def _expand_rope_table(row_ref, col_ref):
    rows = row_ref[...]
    per_token = jnp.broadcast_to(rows[:, None, :], (rows.shape[0], GRID_W, HEAD_DIM))
    return per_token.reshape(col_ref.shape) + col_ref[...]


def _inproj_kernel(x_ref, g_ref, sc_ref, sh_ref, w_ref, qg_ref, kg_ref, *refs, full):
    if full:
        (rcos_ref, rsin_ref, ccos_ref, csin_ref,
         q_ref, k_ref, vt_ref, h_ref, sg_ref, qn_ref, kn_ref) = refs
    else:
        k_ref, vt_ref, kn_ref = refs
    tm = x_ref.shape[0]
    h = (_rms(x_ref[...]) * g_ref[...]) * (1.0 + sc_ref[...]) + sh_ref[...]
    hb = h.astype(BF16)
    lane = lax.broadcasted_iota(jnp.int32, (tm, LANES), 1)
    even = (lane % 2) == 0

    if full:
        cos2 = _expand_rope_table(rcos_ref, ccos_ref)
        sin2 = _expand_rope_table(rsin_ref, csin_ref)
        qa = _dot(hb, w_ref[:, 0:Q_DIM])
        qn = jnp.zeros((tm, LANES), F32)
        scale = HEAD_DIM ** -0.5
        for hd in range(N_HEADS):
            t = _rms(qa[:, hd * HEAD_DIM:(hd + 1) * HEAD_DIM]) * qg_ref[...]
            tb = (_rope(t, cos2, sin2, even) * scale).astype(BF16)
            q_ref[:, hd * HEAD_DIM:(hd + 1) * HEAD_DIM] = tb
            qn = jnp.where(lane == hd, _l2(tb), qn)
        qn_ref[...] = qn

    ka = _dot(hb, w_ref[:, Q_DIM:Q_DIM + KV_DIM])
    kn = jnp.zeros((tm, LANES), F32)
    ones_col = jnp.where(lane == 0, 1.0, 0.0).astype(BF16)
    for g in range(N_KV_HEADS):
        t = _rms(ka[:, g * HEAD_DIM:(g + 1) * HEAD_DIM]) * kg_ref[...]
        if full:
            t = _rope(t, cos2, sin2, even)
        tb = t.astype(BF16)
        k_ref[g, :, 0:HEAD_DIM] = tb
        k_ref[g, :, HEAD_DIM:] = ones_col
        kn = jnp.where(lane == g, _l2(tb), kn)
    kn_ref[...] = jnp.broadcast_to(jnp.max(kn, axis=0, keepdims=True), (SUBLANES_F32, LANES))

    va = _dot(hb, w_ref[:, Q_DIM + KV_DIM:U_OFF])
    vt_ref[...] = va.T.astype(BF16)

    if full:
        ua = _dot(hb, w_ref[:, U_OFF:U_OFF + CONV_DIM])
        ub = _dot(hb, w_ref[:, U_OFF + CONV_DIM:G_OFF])
        h_ref[...] = ua * jax.nn.sigmoid(ub)
        sg_ref[...] = jax.nn.sigmoid(_dot(hb, w_ref[:, G_OFF:IN_DIM]))


def _in_proj(x, g, sc, sh, w_in_b, qg, kg, rope=None, *, tm):
    full = rope is not None
    n = x.shape[0]
    row = lambda i: (i, 0)
    vec = _const_spec((1, D_MODEL))
    hvec = _const_spec((1, HEAD_DIM))
    in_specs = [pl.BlockSpec((tm, D_MODEL), row), vec, vec, vec,
                _const_spec((D_MODEL, IN_DIM)), hvec, hvec]
    operands = [x, g, sc, sh, w_in_b, qg, kg]
    if full:
        grid_rows = tm // GRID_W
        assert grid_rows % SUBLANES_F32 == 0 and rope[2].shape == (tm, HEAD_DIM)
        row_tab = pl.BlockSpec((grid_rows, HEAD_DIM), row)
        in_specs += [row_tab, row_tab, _const_spec((tm, HEAD_DIM)), _const_spec((tm, HEAD_DIM))]
        operands += list(rope)
    k_out = (jax.ShapeDtypeStruct((N_KV_HEADS, n, KP_DIM), BF16),
             pl.BlockSpec((N_KV_HEADS, tm, KP_DIM), lambda i: (0, i, 0)))
    vt_out = (jax.ShapeDtypeStruct((KV_DIM, n), BF16), pl.BlockSpec((KV_DIM, tm), lambda i: (0, i)))
    kn_out = (jax.ShapeDtypeStruct((n // tm * SUBLANES_F32, LANES), F32),
              pl.BlockSpec((SUBLANES_F32, LANES), row))
    if full:
        outs = [(jax.ShapeDtypeStruct((n, Q_DIM), BF16), pl.BlockSpec((tm, Q_DIM), row)),
                k_out, vt_out,
                (jax.ShapeDtypeStruct((n, CONV_DIM), F32), pl.BlockSpec((tm, CONV_DIM), row)),
                (jax.ShapeDtypeStruct((n, 2 * D_MODEL), F32), pl.BlockSpec((tm, 2 * D_MODEL), row)),
                (jax.ShapeDtypeStruct((n, LANES), F32), pl.BlockSpec((tm, LANES), row)),
                kn_out]
    else:
        outs = [k_out, vt_out, kn_out]
    return pl.pallas_call(
        functools.partial(_inproj_kernel, full=full),
        out_shape=[o[0] for o in outs],
        grid=(n // tm,),
        in_specs=in_specs,
        out_specs=[o[1] for o in outs],
        compiler_params=_params(("arbitrary",)),
        name="in_proj" if full else "ctx_proj",
    )(*operands)


def _head_column(ref, hd):
    lane = lax.broadcasted_iota(jnp.int32, ref.shape, 1)
    return jnp.sum(jnp.where(lane == hd, ref[...], 0.0), axis=-1, keepdims=True)


def _shifted_queries(q_ref, shift):
    lane = lax.broadcasted_iota(jnp.int32, (q_ref.shape[0], LANES), 1)
    shift_col = jnp.where(lane == 0, -shift, 0.0).astype(BF16)
    return jnp.concatenate([q_ref[...], shift_col], axis=1)


def _attn_kernel(q_ref, m_ref, ms_ref, kx_ref, vtx_ref, kc_ref, vtc_ref, o_ref):
    hd = pl.program_id(0)
    shift = _head_column(m_ref, hd) * jnp.sum(ms_ref[...], axis=-1, keepdims=True)
    qp = _shifted_queries(q_ref, shift)

    def chunk(k_blk, vt_blk):
        p = jnp.exp(_dot_nt(k_blk, qp))
        psum = jnp.sum(p.reshape(-1, SUBLANES_F32, TQ), axis=0)
        return _dot(vt_blk, p.astype(BF16)), psum

    acc, psum = chunk(kc_ref[...], vtc_ref[...])
    for c in range(SEQ // TK):
        a, s = chunk(kx_ref[c * TK:(c + 1) * TK, :], vtx_ref[:, c * TK:(c + 1) * TK])
        acc = acc + a
        psum = psum + s
    out_t = acc / jnp.sum(psum, axis=0, keepdims=True)
    o_ref[...] = out_t.T.astype(BF16)


def _kv_specs():
    kv = lambda h, i: (h // GROUP, 0, 0)
    one = pl.Buffered(1)
    return [pl.BlockSpec((None, SEQ, KP_DIM), kv, pipeline_mode=one),
            pl.BlockSpec((None, HEAD_DIM, SEQ), kv, pipeline_mode=one),
            pl.BlockSpec((None, CTX_LEN, KP_DIM), kv, pipeline_mode=one),
            pl.BlockSpec((None, HEAD_DIM, CTX_LEN), kv, pipeline_mode=one)]


def _attention(q, m, mscale, kx, vtx, kc, vtc):
    return pl.pallas_call(
        _attn_kernel,
        out_shape=jax.ShapeDtypeStruct((SEQ, Q_DIM), BF16),
        grid=(N_HEADS, SEQ // TQ),
        in_specs=[pl.BlockSpec((TQ, HEAD_DIM), lambda h, i: (i, h)),
                  pl.BlockSpec((TQ, LANES), lambda h, i: (i, 0)),
                  pl.BlockSpec((None, 1, LANES), lambda h, i: (h, 0, 0))] + _kv_specs(),
        out_specs=pl.BlockSpec((TQ, HEAD_DIM), lambda h, i: (i, h)),
        compiler_params=_params(("arbitrary", "arbitrary")),
        name="attention",
    )(q, m, mscale, kx, vtx, kc, vtc)


def _rowmax_kernel(q_ref, kx_ref, kc_ref, o_ref):
    hd = pl.program_id(0)
    qp = _shifted_queries(q_ref, jnp.zeros((q_ref.shape[0], 1), F32))
    m = jnp.max(_dot_nt(qp, kc_ref[...]), axis=-1, keepdims=True)

    def body(c, m):
        k_blk = kx_ref[pl.ds(pl.multiple_of(c * TK, TK), TK), :]
        return jnp.maximum(m, jnp.max(_dot_nt(qp, k_blk), axis=-1, keepdims=True))

    m = lax.fori_loop(0, SEQ // TK, body, m)
    lane = lax.broadcasted_iota(jnp.int32, o_ref.shape[1:], 1)
    o_ref[0] = jnp.where(lane == hd, m, 0.0)


def _row_max(q, kx, kc):
    tq = 256
    kv = lambda h, i: (h // GROUP, 0, 0)
    per_head = pl.pallas_call(
        _rowmax_kernel,
        out_shape=jax.ShapeDtypeStruct((N_HEADS, SEQ, LANES), F32),
        grid=(N_HEADS, SEQ // tq),
        in_specs=[pl.BlockSpec((tq, HEAD_DIM), lambda h, i: (i, h)),
                  pl.BlockSpec((None, SEQ, KP_DIM), kv, pipeline_mode=pl.Buffered(1)),
                  pl.BlockSpec((None, CTX_LEN, KP_DIM), kv, pipeline_mode=pl.Buffered(1))],
        out_specs=pl.BlockSpec((1, tq, LANES), lambda h, i: (h, i, 0)),
        compiler_params=_params(("arbitrary", "arbitrary")),
        name="row_max",
    )(q, kx, kc)
    return jnp.sum(per_head, axis=0)


def _mix_kernel(attn_ref, hm_ref, hp_ref, hn_ref, sg_ref, x_ref, g1_ref,
                wa_ref, wc_ref, wo_ref, dww_ref, dwb_ref, lng_ref, lnb_ref,
                o_ref, hbuf_ref, sbuf_ref):
    i = pl.program_id(0)
    last = pl.num_programs(0) - 1
    tm = x_ref.shape[0]
    hbuf_ref[0:CONV_HALO, :] = jnp.where(i > 0, hp_ref[...], 0.0)
    hbuf_ref[CONV_HALO:CONV_HALO + tm, :] = hm_ref[...]
    hbuf_ref[CONV_HALO + tm:, :] = jnp.where(i < last, hn_ref[...], 0.0)
    span = tm + 2 * CONV_HALO - SUBLANES_F32
    for r in range(1, SUBLANES_F32):
        sbuf_ref[r - 1, 0:span, :] = hbuf_ref[r:r + span, :]

    first = CONV_HALO - CONV_WIDTH // 2
    conv = None
    for k in range(CONV_WIDTH):
        a, r = divmod(first + k, SUBLANES_F32)
        src = hbuf_ref if r == 0 else sbuf_ref.at[r - 1]
        term = src[a * SUBLANES_F32:a * SUBLANES_F32 + tm, :] * dww_ref[k:k + 1, :]
        conv = term + dwb_ref[...] if conv is None else conv + term
    mu = jnp.mean(conv, axis=-1, keepdims=True)
    cen = conv - mu
    var = jnp.mean(cen * cen, axis=-1, keepdims=True)
    ln = cen * lax.rsqrt(var + EPS) * lng_ref[...] + lnb_ref[...]
    act = ln * jax.nn.sigmoid(ln)
    y_b = _dot(act.astype(BF16), wc_ref[...])
    y_a = _dot(attn_ref[...], wa_ref[...])
    merged = sg_ref[:, 0:D_MODEL] * y_a + sg_ref[:, D_MODEL:] * y_b
    o_ref[...] = x_ref[...] + g1_ref[...] * _dot(merged.astype(BF16), wo_ref[...])


def _halo_maps(tm, halo):
    nblk = tm // halo
    n_halo_blocks = SEQ // halo
    prev = lambda i: (jnp.maximum(i * nblk - 1, 0), 0)
    nxt = lambda i: (jnp.minimum((i + 1) * nblk, n_halo_blocks - 1), 0)
    return prev, nxt


def _mix(attn, hglu, sg, x, g1, wa, wc, wo, dww, dwb, lng, lnb):
    tm = TM_MIX
    row = lambda i: (i, 0)
    prev, nxt = _halo_maps(tm, CONV_HALO)
    cvec = _const_spec((1, CONV_DIM))
    ext = tm + 2 * CONV_HALO
    return pl.pallas_call(
        _mix_kernel,
        out_shape=jax.ShapeDtypeStruct((SEQ, D_MODEL), F32),
        grid=(SEQ // tm,),
        in_specs=[pl.BlockSpec((tm, Q_DIM), row),
                  pl.BlockSpec((tm, CONV_DIM), row),
                  pl.BlockSpec((CONV_HALO, CONV_DIM), prev),
                  pl.BlockSpec((CONV_HALO, CONV_DIM), nxt),
                  pl.BlockSpec((tm, 2 * D_MODEL), row),
                  pl.BlockSpec((tm, D_MODEL), row),
                  _const_spec((1, D_MODEL)),
                  _const_spec((Q_DIM, D_MODEL)), _const_spec((CONV_DIM, D_MODEL)),
                  _const_spec((D_MODEL, D_MODEL)),
                  _const_spec((CONV_WIDTH, CONV_DIM)), cvec, cvec, cvec],
        out_specs=pl.BlockSpec((tm, D_MODEL), row),
        scratch_shapes=[pltpu.VMEM((ext, CONV_DIM), F32),
                        pltpu.VMEM((SUBLANES_F32 - 1, ext, CONV_DIM), F32)],
        compiler_params=_params(("arbitrary",)),
        name="mix",
    )(attn, hglu, hglu, hglu, sg, x, g1, wa, wc, wo, dww, dwb, lng, lnb)


def _ffn_kernel(xm_ref, xp_ref, xn_ref, ng_ref, sc_ref, sh_ref, g2_ref, fg_ref,
                wup_ref, dw_ref, wdn_ref, o_ref, xbuf_ref, abuf_ref, act_ref):
    i = pl.program_id(0)
    tm = xm_ref.shape[0]
    ext = tm + 2 * FFN_HALO
    xbuf_ref[0:FFN_HALO, :] = xp_ref[...]
    xbuf_ref[FFN_HALO:FFN_HALO + tm, :] = xm_ref[...]
    xbuf_ref[FFN_HALO + tm:, :] = xn_ref[...]
    h = (_rms(xbuf_ref[...]) * ng_ref[...]) * (1.0 + sc_ref[...]) + sh_ref[...]
    hb = h.astype(BF16)
    grow = lax.broadcasted_iota(jnp.int32, (ext, 1), 0) + (i * tm - FFN_HALO)
    valid = (grow >= 0) & (grow < SEQ)
    for off in range(0, FFN_DIM, FFN_CHUNK):
        cols = slice(off, off + FFN_CHUNK)
        a = _dot(hb, wup_ref[:, cols])
        abuf_ref[:, cols] = jnp.where(valid, a, 0.0)
        b = _dot(hb[FFN_HALO:FFN_HALO + tm, :], wup_ref[:, FFN_DIM + off:FFN_DIM + off + FFN_CHUNK])
        conv = (abuf_ref[FFN_HALO - 1:FFN_HALO - 1 + tm, cols] * dw_ref[0:1, cols]
                + abuf_ref[FFN_HALO:FFN_HALO + tm, cols] * dw_ref[1:2, cols]
                + abuf_ref[FFN_HALO + 1:FFN_HALO + 1 + tm, cols] * dw_ref[2:3, cols]
                + dw_ref[3:4, cols])
        act_ref[:, cols] = (jax.nn.gelu(conv, approximate=True) * b).astype(BF16)
    x2 = xm_ref[...] + g2_ref[...] * _dot(act_ref[...], wdn_ref[...])
    o_ref[...] = _rms(x2) * fg_ref[...]


def _ffn(x1, ng, sc, sh, g2, fg, wup, dw, wdn):
    tm = TM_FFN
    row = lambda i: (i, 0)
    prev, nxt = _halo_maps(tm, FFN_HALO)
    vec = _const_spec((1, D_MODEL))
    ext = tm + 2 * FFN_HALO
    return pl.pallas_call(
        _ffn_kernel,
        out_shape=jax.ShapeDtypeStruct((SEQ, D_MODEL), F32),
        grid=(SEQ // tm,),
        in_specs=[pl.BlockSpec((tm, D_MODEL), row),
                  pl.BlockSpec((FFN_HALO, D_MODEL), prev),
                  pl.BlockSpec((FFN_HALO, D_MODEL), nxt),
                  vec, vec, vec, vec, vec,
                  _const_spec((D_MODEL, 2 * FFN_DIM)),
                  _const_spec((SUBLANES_F32, FFN_DIM)),
                  _const_spec((FFN_DIM, D_MODEL))],
        out_specs=pl.BlockSpec((tm, D_MODEL), row),
        scratch_shapes=[pltpu.VMEM((ext, D_MODEL), F32),
                        pltpu.VMEM((ext, FFN_DIM), F32),
                        pltpu.VMEM((tm, FFN_DIM), BF16)],
        compiler_params=_params(("arbitrary",)),
        name="ffn",
    )(x1, x1, x1, ng, sc, sh, g2, fg, wup, dw, wdn)


def _rope_tables(tm):
    half = HEAD_DIM // 2
    rows = SEQ // GRID_W
    inv_freq = ROPE_THETA ** (-jnp.arange(0, half, 2, dtype=F32) / half)
    row_ang = jnp.arange(rows, dtype=F32)[:, None] * inv_freq
    col_ang = jnp.arange(GRID_W, dtype=F32)[:, None] * inv_freq

    def pair_tables(ang):
        c, s = jnp.cos(ang), jnp.sin(ang)
        return jnp.repeat(c, 2, axis=-1), jnp.stack([-s, s], axis=-1).reshape(ang.shape[0], half)

    row_tabs = [jnp.pad(t, ((0, 0), (0, half))) for t in pair_tables(row_ang)]
    col_tabs = [jnp.tile(jnp.pad(t, ((0, 0), (half, 0))), (tm // GRID_W, 1))
                for t in pair_tables(col_ang)]
    return row_tabs + col_tabs


def kernel(x, c, ctx, c_ctx, w_mod, b_mod, norm1_g, w_in, q_norm_g, k_norm_g, w_attn_out,
           conv_dw_w, conv_dw_b, conv_ln_g, conv_ln_b, w_conv_out, w_out, norm2_g,
           w_up, ffn_dw_w, ffn_dw_b, w_down, final_g):
    assert x.shape == (1, SEQ, D_MODEL) and ctx.shape == (1, CTX_LEN, D_MODEL)
    assert w_mod.shape[0] == 1, "single layer"
    x2d = x[0]
    ctx2d = ctx[0]
    r1 = lambda v: v.reshape(1, -1)

    cvecs = jnp.zeros((SUBLANES_F32, D_MODEL), F32).at[0].set(c[0]).at[1].set(c_ctx)
    mod = _modulation(cvecs, w_mod[0], r1(b_mod[0]))
    sh1, sc1, g1, sh2, sc2, g2 = [r1(mod[0, j * D_MODEL:(j + 1) * D_MODEL]) for j in range(6)]
    csh1, csc1 = [r1(mod[1, j * D_MODEL:(j + 1) * D_MODEL]) for j in range(2)]

    w_in_b = w_in[0].astype(BF16)
    ng1, qg, kg = r1(norm1_g[0]), r1(q_norm_g[0]), r1(k_norm_g[0])
    q, kx, vtx, hglu, sg, qn, knx = _in_proj(x2d, ng1, sc1, sh1, w_in_b, qg, kg,
                                            _rope_tables(TM_PROJ), tm=TM_PROJ)
    kc, vtc, knc = _in_proj(ctx2d, ng1, csc1, csh1, w_in_b, qg, kg, tm=CTX_LEN)
    vtx = vtx.reshape(N_KV_HEADS, HEAD_DIM, SEQ)
    vtc = vtc.reshape(N_KV_HEADS, HEAD_DIM, CTX_LEN)

    kmax = jnp.maximum(jnp.max(knx, axis=0), jnp.max(knc, axis=0))[:N_KV_HEADS]
    kmax_head = jnp.repeat(kmax, GROUP)
    qmax = jnp.max(qn, axis=0)[:N_HEADS]
    lane0 = jnp.zeros((N_HEADS, 1, LANES), F32).at[:, 0, 0]
    m, mscale = lax.cond(
        jnp.max(qmax * kmax_head) > SAFE_SHIFT,
        lambda: (_row_max(q, kx, kc), lane0.set(1.0)),
        lambda: (qn, lane0.set(kmax_head)))
    attn = _attention(q, m, mscale, kx, vtx, kc, vtc)

    x1 = _mix(attn, hglu, sg, x2d, g1,
              w_attn_out[0].astype(BF16), w_conv_out[0].astype(BF16), w_out[0].astype(BF16),
              conv_dw_w[0], r1(conv_dw_b[0]), r1(conv_ln_g[0]), r1(conv_ln_b[0]))

    dw = jnp.concatenate([ffn_dw_w[0], ffn_dw_b[0][None, :],
                          jnp.zeros((SUBLANES_F32 - 4, FFN_DIM), F32)], axis=0)
    out = _ffn(x1, r1(norm2_g[0]), sc2, sh2, g2, r1(final_g),
               w_up[0].astype(BF16), dw, w_down[0].astype(BF16))
    return out[None]
```

```python
import functools

import jax
import jax.numpy as jnp
from jax import lax
from jax.experimental import pallas as pl
from jax.experimental.pallas import tpu as pltpu

D_MODEL = 1024
SEQ = 16384
GRID_W = 64
CTX_LEN = 256
N_HEADS = 8
N_KV_HEADS = 2
HEAD_DIM = 128
ROPE_THETA = 10000.0
CONV_DIM = 512
CONV_WIDTH = 31
FFN_DIM = 2816
EPS = 1e-6
Q_DIM = N_HEADS * HEAD_DIM
KV_DIM = N_KV_HEADS * HEAD_DIM
GROUP = N_HEADS // N_KV_HEADS
U_OFF = Q_DIM + 2 * KV_DIM
G_OFF = U_OFF + 2 * CONV_DIM
IN_DIM = G_OFF + 2 * D_MODEL

F32 = jnp.float32
BF16 = jnp.bfloat16

LANES = 128
SUBLANES_F32 = 8
MXU_DIM = 256
VMEM_LIMIT_BYTES = 56 * 1024 * 1024

TM_PROJ = 512
TQ = 2048
TK = 4096
KP_DIM = 2 * HEAD_DIM
TM_MIX = 512
CONV_HALO = 16
TM_FFN = 512
FFN_HALO = SUBLANES_F32
FFN_CHUNK = MXU_DIM
SAFE_SHIFT = 40.0

assert SEQ % TK == 0 and SEQ % TQ == 0 and SEQ % TM_PROJ == 0 and CTX_LEN % MXU_DIM == 0
assert FFN_DIM % FFN_CHUNK == 0 and CONV_HALO >= CONV_WIDTH // 2


def _dot(a, b):
    return jnp.dot(a, b, preferred_element_type=F32)


def _dot_nt(a, b):
    return lax.dot_general(a, b, (((1,), (1,)), ((), ())), preferred_element_type=F32)


def _rms(x):
    return x * lax.rsqrt(jnp.mean(x * x, axis=-1, keepdims=True) + EPS)


def _const_spec(shape):
    return pl.BlockSpec(shape, lambda *_: (0,) * len(shape), pipeline_mode=pl.Buffered(1))


def _params(sem):
    return pltpu.CompilerParams(dimension_semantics=sem, vmem_limit_bytes=VMEM_LIMIT_BYTES)


def _mod_kernel(c_ref, w_ref, b_ref, o_ref):
    c = c_ref[...]
    s = c * jax.nn.sigmoid(c)
    w = w_ref[...]
    s_hi = s.astype(BF16)
    s_lo = (s - s_hi.astype(F32)).astype(BF16)
    w_hi = w.astype(BF16)
    w_lo = (w - w_hi.astype(F32)).astype(BF16)
    o_ref[...] = _dot(s_hi, w_hi) + _dot(s_hi, w_lo) + _dot(s_lo, w_hi) + b_ref[...]


def _modulation(cvecs, w_mod, b_mod):
    rows, n_out = cvecs.shape[0], w_mod.shape[1]
    bn = 1024
    return pl.pallas_call(
        _mod_kernel,
        out_shape=jax.ShapeDtypeStruct((rows, n_out), F32),
        grid=(n_out // bn,),
        in_specs=[pl.BlockSpec((rows, D_MODEL), lambda j: (0, 0)),
                  pl.BlockSpec((D_MODEL, bn), lambda j: (0, j)),
                  pl.BlockSpec((1, bn), lambda j: (0, j))],
        out_specs=pl.BlockSpec((rows, bn), lambda j: (0, j)),
        compiler_params=_params(("arbitrary",)),
        name="modulation",
    )(cvecs, w_mod, b_mod)


def _rope(t, cos2, sin2, even):
    swapped = jnp.where(even, pltpu.roll(t, HEAD_DIM - 1, 1), pltpu.roll(t, 1, 1))
    return t * cos2 + swapped * sin2


def _l2(tb):
    t = tb.astype(F32)
    return jnp.sqrt(jnp.sum(t * t, axis=-1, keepdims=True))


def _expand_rope_table(row_ref, col_ref):
    rows = row_ref[...]
    per_token = jnp.broadcast_to(rows[:, None, :], (rows.shape[0], GRID_W, HEAD_DIM))
    return per_token.reshape(col_ref.shape) + col_ref[...]


def _inproj_kernel(x_ref, g_ref, sc_ref, sh_ref, w_ref, qg_ref, kg_ref, *refs, full):
    if full:
        (rcos_ref, rsin_ref, ccos_ref, csin_ref,
         q_ref, k_ref, vt_ref, h_ref, sg_ref, qn_ref, qmax_ref, kn_ref) = refs
    else:
        k_ref, vt_ref, kn_ref = refs
    tm = x_ref.shape[0]
    h = (_rms(x_ref[...]) * g_ref[...]) * (1.0 + sc_ref[...]) + sh_ref[...]
    hb = h.astype(BF16)
    lane = lax.broadcasted_iota(jnp.int32, (tm, LANES), 1)
    even = (lane % 2) == 0

    if full:
        cos2 = _expand_rope_table(rcos_ref, ccos_ref)
        sin2 = _expand_rope_table(rsin_ref, csin_ref)
        qa = _dot(hb, w_ref[:, 0:Q_DIM])
        qn = jnp.zeros((tm, LANES), F32)
        scale = HEAD_DIM ** -0.5
        for hd in range(N_HEADS):
            t = _rms(qa[:, hd * HEAD_DIM:(hd + 1) * HEAD_DIM]) * qg_ref[...]
            tb = (_rope(t, cos2, sin2, even) * scale).astype(BF16)
            q_ref[:, hd * HEAD_DIM:(hd + 1) * HEAD_DIM] = tb
            qn = jnp.where(lane == hd, _l2(tb), qn)
        qn_ref[...] = qn
        qmax_ref[...] = jnp.broadcast_to(jnp.max(qn, axis=0, keepdims=True), (SUBLANES_F32, LANES))

    ka = _dot(hb, w_ref[:, Q_DIM:Q_DIM + KV_DIM])
    kn = jnp.zeros((tm, LANES), F32)
    ones_col = jnp.where(lane == 0, 1.0, 0.0).astype(BF16)
    for g in range(N_KV_HEADS):
        t = _rms(ka[:, g * HEAD_DIM:(g + 1) * HEAD_DIM]) * kg_ref[...]
        if full:
            t = _rope(t, cos2, sin2, even)
        tb = t.astype(BF16)
        k_ref[g, :, 0:HEAD_DIM] = tb
        k_ref[g, :, HEAD_DIM:] = ones_col
        kn = jnp.where(lane == g, _l2(tb), kn)
    kn_ref[...] = jnp.broadcast_to(jnp.max(kn, axis=0, keepdims=True), (SUBLANES_F32, LANES))

    va = _dot(hb, w_ref[:, Q_DIM + KV_DIM:U_OFF])
    vt_ref[...] = va.T.astype(BF16)

    if full:
        ua = _dot(hb, w_ref[:, U_OFF:U_OFF + CONV_DIM])
        ub = _dot(hb, w_ref[:, U_OFF + CONV_DIM:G_OFF])
        h_ref[...] = ua * jax.nn.sigmoid(ub)
        sg_ref[...] = jax.nn.sigmoid(_dot(hb, w_ref[:, G_OFF:IN_DIM]))


def _in_proj(x, g, sc, sh, w_in_b, qg, kg, rope=None, *, tm):
    full = rope is not None
    n = x.shape[0]
    row = lambda i: (i, 0)
    vec = _const_spec((1, D_MODEL))
    hvec = _const_spec((1, HEAD_DIM))
    in_specs = [pl.BlockSpec((tm, D_MODEL), row), vec, vec, vec,
                _const_spec((D_MODEL, IN_DIM)), hvec, hvec]
    operands = [x, g, sc, sh, w_in_b, qg, kg]
    if full:
        grid_rows = tm // GRID_W
        assert grid_rows % SUBLANES_F32 == 0 and rope[2].shape == (tm, HEAD_DIM)
        row_tab = pl.BlockSpec((grid_rows, HEAD_DIM), row)
        in_specs += [row_tab, row_tab, _const_spec((tm, HEAD_DIM)), _const_spec((tm, HEAD_DIM))]
        operands += list(rope)
    k_out = (jax.ShapeDtypeStruct((N_KV_HEADS, n, KP_DIM), BF16),
             pl.BlockSpec((N_KV_HEADS, tm, KP_DIM), lambda i: (0, i, 0)))
    vt_out = (jax.ShapeDtypeStruct((KV_DIM, n), BF16), pl.BlockSpec((KV_DIM, tm), lambda i: (0, i)))
    tile_max = (jax.ShapeDtypeStruct((n // tm * SUBLANES_F32, LANES), F32),
                pl.BlockSpec((SUBLANES_F32, LANES), row))
    kn_out = tile_max
    if full:
        outs = [(jax.ShapeDtypeStruct((n, Q_DIM), BF16), pl.BlockSpec((tm, Q_DIM), row)),
                k_out, vt_out,
                (jax.ShapeDtypeStruct((n, CONV_DIM), F32), pl.BlockSpec((tm, CONV_DIM), row)),
                (jax.ShapeDtypeStruct((n, 2 * D_MODEL), F32), pl.BlockSpec((tm, 2 * D_MODEL), row)),
                (jax.ShapeDtypeStruct((n, LANES), F32), pl.BlockSpec((tm, LANES), row)),
                tile_max, kn_out]
    else:
        outs = [k_out, vt_out, kn_out]
    return pl.pallas_call(
        functools.partial(_inproj_kernel, full=full),
        out_shape=[o[0] for o in outs],
        grid=(n // tm,),
        in_specs=in_specs,
        out_specs=[o[1] for o in outs],
        compiler_params=_params(("arbitrary",)),
        name="in_proj" if full else "ctx_proj",
    )(*operands)


def _head_column(ref, hd):
    lane = lax.broadcasted_iota(jnp.int32, ref.shape, 1)
    return jnp.sum(jnp.where(lane == hd, ref[...], 0.0), axis=-1, keepdims=True)


def _shifted_queries(q_ref, shift):
    lane = lax.broadcasted_iota(jnp.int32, (q_ref.shape[0], LANES), 1)
    shift_col = jnp.where(lane == 0, -shift, 0.0).astype(BF16)
    return jnp.concatenate([q_ref[...], shift_col], axis=1)


def _attn_kernel(q_ref, m_ref, ms_ref, kx_ref, vtx_ref, kc_ref, vtc_ref, o_ref):
    hd = pl.program_id(0)
    shift = _head_column(m_ref, hd) * jnp.sum(ms_ref[...], axis=-1, keepdims=True)
    qp = _shifted_queries(q_ref, shift)

    def chunk(k_blk, vt_blk):
        p = jnp.exp(_dot_nt(k_blk, qp))
        psum = jnp.sum(p.reshape(-1, SUBLANES_F32, TQ), axis=0)
        return _dot(vt_blk, p.astype(BF16)), psum

    acc, psum = chunk(kc_ref[...], vtc_ref[...])
    for c in range(SEQ // TK):
        a, s = chunk(kx_ref[c * TK:(c + 1) * TK, :], vtx_ref[:, c * TK:(c + 1) * TK])
        acc = acc + a
        psum = psum + s
    out_t = acc / jnp.sum(psum, axis=0, keepdims=True)
    o_ref[...] = out_t.T.astype(BF16)


def _kv_specs():
    kv = lambda h, i: (h // GROUP, 0, 0)
    one = pl.Buffered(1)
    return [pl.BlockSpec((None, SEQ, KP_DIM), kv, pipeline_mode=one),
            pl.BlockSpec((None, HEAD_DIM, SEQ), kv, pipeline_mode=one),
            pl.BlockSpec((None, CTX_LEN, KP_DIM), kv, pipeline_mode=one),
            pl.BlockSpec((None, HEAD_DIM, CTX_LEN), kv, pipeline_mode=one)]


def _attention(q, m, mscale, kx, vtx, kc, vtc):
    return pl.pallas_call(
        _attn_kernel,
        out_shape=jax.ShapeDtypeStruct((SEQ, Q_DIM), BF16),
        grid=(N_HEADS, SEQ // TQ),
        in_specs=[pl.BlockSpec((TQ, HEAD_DIM), lambda h, i: (i, h)),
                  pl.BlockSpec((TQ, LANES), lambda h, i: (i, 0)),
                  pl.BlockSpec((None, 1, LANES), lambda h, i: (h, 0, 0))] + _kv_specs(),
        out_specs=pl.BlockSpec((TQ, HEAD_DIM), lambda h, i: (i, h)),
        compiler_params=_params(("arbitrary", "arbitrary")),
        name="attention",
    )(q, m, mscale, kx, vtx, kc, vtc)


def _rowmax_kernel(q_ref, kx_ref, kc_ref, o_ref):
    hd = pl.program_id(0)
    qp = _shifted_queries(q_ref, jnp.zeros((q_ref.shape[0], 1), F32))
    m = jnp.max(_dot_nt(qp, kc_ref[...]), axis=-1, keepdims=True)

    def body(c, m):
        k_blk = kx_ref[pl.ds(pl.multiple_of(c * TK, TK), TK), :]
        return jnp.maximum(m, jnp.max(_dot_nt(qp, k_blk), axis=-1, keepdims=True))

    m = lax.fori_loop(0, SEQ // TK, body, m)
    lane = lax.broadcasted_iota(jnp.int32, o_ref.shape[1:], 1)
    o_ref[0] = jnp.where(lane == hd, m, 0.0)


def _row_max(q, kx, kc):
    tq = 256
    kv = lambda h, i: (h // GROUP, 0, 0)
    per_head = pl.pallas_call(
        _rowmax_kernel,
        out_shape=jax.ShapeDtypeStruct((N_HEADS, SEQ, LANES), F32),
        grid=(N_HEADS, SEQ // tq),
        in_specs=[pl.BlockSpec((tq, HEAD_DIM), lambda h, i: (i, h)),
                  pl.BlockSpec((None, SEQ, KP_DIM), kv, pipeline_mode=pl.Buffered(1)),
                  pl.BlockSpec((None, CTX_LEN, KP_DIM), kv, pipeline_mode=pl.Buffered(1))],
        out_specs=pl.BlockSpec((1, tq, LANES), lambda h, i: (h, i, 0)),
        compiler_params=_params(("arbitrary", "arbitrary")),
        name="row_max",
    )(q, kx, kc)
    return jnp.sum(per_head, axis=0)


def _mix_kernel(attn_ref, hm_ref, hp_ref, hn_ref, sg_ref, x_ref, g1_ref,
                wa_ref, wc_ref, wo_ref, dww_ref, dwb_ref, lng_ref, lnb_ref,
                o_ref, hbuf_ref, sbuf_ref):
    i = pl.program_id(0)
    last = pl.num_programs(0) - 1
    tm = x_ref.shape[0]
    hbuf_ref[0:CONV_HALO, :] = jnp.where(i > 0, hp_ref[...], 0.0)
    hbuf_ref[CONV_HALO:CONV_HALO + tm, :] = hm_ref[...]
    hbuf_ref[CONV_HALO + tm:, :] = jnp.where(i < last, hn_ref[...], 0.0)
    span = tm + 2 * CONV_HALO - SUBLANES_F32
    for r in range(1, SUBLANES_F32):
        sbuf_ref[r - 1, 0:span, :] = hbuf_ref[r:r + span, :]

    first = CONV_HALO - CONV_WIDTH // 2
    conv = None
    for k in range(CONV_WIDTH):
        a, r = divmod(first + k, SUBLANES_F32)
        src = hbuf_ref if r == 0 else sbuf_ref.at[r - 1]
        term = src[a * SUBLANES_F32:a * SUBLANES_F32 + tm, :] * dww_ref[k:k + 1, :]
        conv = term + dwb_ref[...] if conv is None else conv + term
    mu = jnp.mean(conv, axis=-1, keepdims=True)
    cen = conv - mu
    var = jnp.mean(cen * cen, axis=-1, keepdims=True)
    ln = cen * lax.rsqrt(var + EPS) * lng_ref[...] + lnb_ref[...]
    act = ln * jax.nn.sigmoid(ln)
    y_b = _dot(act.astype(BF16), wc_ref[...])
    y_a = _dot(attn_ref[...], wa_ref[...])
    merged = sg_ref[:, 0:D_MODEL] * y_a + sg_ref[:, D_MODEL:] * y_b
    o_ref[...] = x_ref[...] + g1_ref[...] * _dot(merged.astype(BF16), wo_ref[...])


def _halo_maps(tm, halo):
    nblk = tm // halo
    n_halo_blocks = SEQ // halo
    prev = lambda i: (jnp.maximum(i * nblk - 1, 0), 0)
    nxt = lambda i: (jnp.minimum((i + 1) * nblk, n_halo_blocks - 1), 0)
    return prev, nxt


def _mix(attn, hglu, sg, x, g1, wa, wc, wo, dww, dwb, lng, lnb):
    tm = TM_MIX
    row = lambda i: (i, 0)
    prev, nxt = _halo_maps(tm, CONV_HALO)
    cvec = _const_spec((1, CONV_DIM))
    ext = tm + 2 * CONV_HALO
    return pl.pallas_call(
        _mix_kernel,
        out_shape=jax.ShapeDtypeStruct((SEQ, D_MODEL), F32),
        grid=(SEQ // tm,),
        in_specs=[pl.BlockSpec((tm, Q_DIM), row),
                  pl.BlockSpec((tm, CONV_DIM), row),
                  pl.BlockSpec((CONV_HALO, CONV_DIM), prev),
                  pl.BlockSpec((CONV_HALO, CONV_DIM), nxt),
                  pl.BlockSpec((tm, 2 * D_MODEL), row),
                  pl.BlockSpec((tm, D_MODEL), row),
                  _const_spec((1, D_MODEL)),
                  _const_spec((Q_DIM, D_MODEL)), _const_spec((CONV_DIM, D_MODEL)),
                  _const_spec((D_MODEL, D_MODEL)),
                  _const_spec((CONV_WIDTH, CONV_DIM)), cvec, cvec, cvec],
        out_specs=pl.BlockSpec((tm, D_MODEL), row),
        scratch_shapes=[pltpu.VMEM((ext, CONV_DIM), F32),
                        pltpu.VMEM((SUBLANES_F32 - 1, ext, CONV_DIM), F32)],
        compiler_params=_params(("arbitrary",)),
        name="mix",
    )(attn, hglu, hglu, hglu, sg, x, g1, wa, wc, wo, dww, dwb, lng, lnb)


def _ffn_kernel(xm_ref, xp_ref, xn_ref, ng_ref, sc_ref, sh_ref, g2_ref, fg_ref,
                wup_ref, dw_ref, wdn_ref, o_ref, xbuf_ref, abuf_ref, act_ref):
    i = pl.program_id(0)
    tm = xm_ref.shape[0]
    ext = tm + 2 * FFN_HALO
    xbuf_ref[0:FFN_HALO, :] = xp_ref[...]
    xbuf_ref[FFN_HALO:FFN_HALO + tm, :] = xm_ref[...]
    xbuf_ref[FFN_HALO + tm:, :] = xn_ref[...]
    h = (_rms(xbuf_ref[...]) * ng_ref[...]) * (1.0 + sc_ref[...]) + sh_ref[...]
    hb = h.astype(BF16)
    grow = lax.broadcasted_iota(jnp.int32, (ext, 1), 0) + (i * tm - FFN_HALO)
    valid = (grow >= 0) & (grow < SEQ)
    for off in range(0, FFN_DIM, FFN_CHUNK):
        cols = slice(off, off + FFN_CHUNK)
        a = _dot(hb, wup_ref[:, cols])
        abuf_ref[:, cols] = jnp.where(valid, a, 0.0)
        b = _dot(hb[FFN_HALO:FFN_HALO + tm, :], wup_ref[:, FFN_DIM + off:FFN_DIM + off + FFN_CHUNK])
        conv = (abuf_ref[FFN_HALO - 1:FFN_HALO - 1 + tm, cols] * dw_ref[0:1, cols]
                + abuf_ref[FFN_HALO:FFN_HALO + tm, cols] * dw_ref[1:2, cols]
                + abuf_ref[FFN_HALO + 1:FFN_HALO + 1 + tm, cols] * dw_ref[2:3, cols]
                + dw_ref[3:4, cols])
        act_ref[:, cols] = (jax.nn.gelu(conv, approximate=True) * b).astype(BF16)
    x2 = xm_ref[...] + g2_ref[...] * _dot(act_ref[...], wdn_ref[...])
    o_ref[...] = _rms(x2) * fg_ref[...]


def _ffn(x1, ng, sc, sh, g2, fg, wup, dw, wdn):
    tm = TM_FFN
    row = lambda i: (i, 0)
    prev, nxt = _halo_maps(tm, FFN_HALO)
    vec = _const_spec((1, D_MODEL))
    ext = tm + 2 * FFN_HALO
    return pl.pallas_call(
        _ffn_kernel,
        out_shape=jax.ShapeDtypeStruct((SEQ, D_MODEL), F32),
        grid=(SEQ // tm,),
        in_specs=[pl.BlockSpec((tm, D_MODEL), row),
                  pl.BlockSpec((FFN_HALO, D_MODEL), prev),
                  pl.BlockSpec((FFN_HALO, D_MODEL), nxt),
                  vec, vec, vec, vec, vec,
                  _const_spec((D_MODEL, 2 * FFN_DIM)),
                  _const_spec((SUBLANES_F32, FFN_DIM)),
                  _const_spec((FFN_DIM, D_MODEL))],
        out_specs=pl.BlockSpec((tm, D_MODEL), row),
        scratch_shapes=[pltpu.VMEM((ext, D_MODEL), F32),
                        pltpu.VMEM((ext, FFN_DIM), F32),
                        pltpu.VMEM((tm, FFN_DIM), BF16)],
        compiler_params=_params(("arbitrary",)),
        name="ffn",
    )(x1, x1, x1, ng, sc, sh, g2, fg, wup, dw, wdn)


def _rope_tables(tm):
    half = HEAD_DIM // 2
    rows = SEQ // GRID_W
    inv_freq = ROPE_THETA ** (-jnp.arange(0, half, 2, dtype=F32) / half)
    row_ang = jnp.arange(rows, dtype=F32)[:, None] * inv_freq
    col_ang = jnp.arange(GRID_W, dtype=F32)[:, None] * inv_freq

    def pair_tables(ang):
        c, s = jnp.cos(ang), jnp.sin(ang)
        return jnp.repeat(c, 2, axis=-1), jnp.stack([-s, s], axis=-1).reshape(ang.shape[0], half)

    row_tabs = [jnp.pad(t, ((0, 0), (0, half))) for t in pair_tables(row_ang)]
    col_tabs = [jnp.tile(jnp.pad(t, ((0, 0), (half, 0))), (tm // GRID_W, 1))
                for t in pair_tables(col_ang)]
    return row_tabs + col_tabs


def kernel(x, c, ctx, c_ctx, w_mod, b_mod, norm1_g, w_in, q_norm_g, k_norm_g, w_attn_out,
           conv_dw_w, conv_dw_b, conv_ln_g, conv_ln_b, w_conv_out, w_out, norm2_g,
           w_up, ffn_dw_w, ffn_dw_b, w_down, final_g):
    assert x.shape == (1, SEQ, D_MODEL) and ctx.shape == (1, CTX_LEN, D_MODEL)
    assert w_mod.shape[0] == 1, "single layer"
    x2d = x[0]
    ctx2d = ctx[0]
    r1 = lambda v: v.reshape(1, -1)

    cvecs = jnp.zeros((SUBLANES_F32, D_MODEL), F32).at[0].set(c[0]).at[1].set(c_ctx)
    mod = _modulation(cvecs, w_mod[0], r1(b_mod[0]))
    sh1, sc1, g1, sh2, sc2, g2 = [r1(mod[0, j * D_MODEL:(j + 1) * D_MODEL]) for j in range(6)]
    csh1, csc1 = [r1(mod[1, j * D_MODEL:(j + 1) * D_MODEL]) for j in range(2)]

    w_in_b = w_in[0].astype(BF16)
    ng1, qg, kg = r1(norm1_g[0]), r1(q_norm_g[0]), r1(k_norm_g[0])
    q, kx, vtx, hglu, sg, qn, qtmax, knx = _in_proj(x2d, ng1, sc1, sh1, w_in_b, qg, kg,
                                                   _rope_tables(TM_PROJ), tm=TM_PROJ)
    kc, vtc, knc = _in_proj(ctx2d, ng1, csc1, csh1, w_in_b, qg, kg, tm=CTX_LEN)
    vtx = vtx.reshape(N_KV_HEADS, HEAD_DIM, SEQ)
    vtc = vtc.reshape(N_KV_HEADS, HEAD_DIM, CTX_LEN)

    kmax = jnp.maximum(jnp.max(knx, axis=0), jnp.max(knc, axis=0))[:N_KV_HEADS]
    kmax_head = jnp.repeat(kmax, GROUP)
    qmax = jnp.max(qtmax, axis=0)[:N_HEADS]
    lane0 = jnp.zeros((N_HEADS, 1, LANES), F32).at[:, 0, 0]
    m, mscale = lax.cond(
        jnp.max(qmax * kmax_head) > SAFE_SHIFT,
        lambda: (_row_max(q, kx, kc), lane0.set(1.0)),
        lambda: (qn, lane0.set(kmax_head)))
    attn = _attention(q, m, mscale, kx, vtx, kc, vtc)

    x1 = _mix(attn, hglu, sg, x2d, g1,
              w_attn_out[0].astype(BF16), w_conv_out[0].astype(BF16), w_out[0].astype(BF16),
              conv_dw_w[0], r1(conv_dw_b[0]), r1(conv_ln_g[0]), r1(conv_ln_b[0]))

    dw = jnp.concatenate([ffn_dw_w[0], ffn_dw_b[0][None, :],
                          jnp.zeros((SUBLANES_F32 - 4, FFN_DIM), F32)], axis=0)
    out = _ffn(x1, r1(norm2_g[0]), sc2, sh2, g2, r1(final_g),
               w_up[0].astype(BF16), dw, w_down[0].astype(BF16))
    return out[None]
```

```python
import functools

import jax
import jax.numpy as jnp
from jax import lax
from jax.experimental import pallas as pl
from jax.experimental.pallas import tpu as pltpu

D_MODEL = 1024
SEQ = 16384
GRID_W = 64
CTX_LEN = 256
N_HEADS = 8
N_KV_HEADS = 2
HEAD_DIM = 128
ROPE_THETA = 10000.0
CONV_DIM = 512
CONV_WIDTH = 31
FFN_DIM = 2816
EPS = 1e-6
Q_DIM = N_HEADS * HEAD_DIM
KV_DIM = N_KV_HEADS * HEAD_DIM
GROUP = N_HEADS // N_KV_HEADS
U_OFF = Q_DIM + 2 * KV_DIM
G_OFF = U_OFF + 2 * CONV_DIM
IN_DIM = G_OFF + 2 * D_MODEL

F32 = jnp.float32
BF16 = jnp.bfloat16

LANES = 128
SUBLANES_F32 = 8
MXU_DIM = 256
VMEM_LIMIT_BYTES = 56 * 1024 * 1024

TM_PROJ = 512
TQ = 2048
TK = 4096
KP_DIM = 2 * HEAD_DIM
TM_MIX = 512
CONV_HALO = 16
TM_FFN = 512
FFN_HALO = SUBLANES_F32
FFN_CHUNK = MXU_DIM
SAFE_SHIFT = 40.0
SHIFT_MARGIN = 1.02

assert SEQ % TK == 0 and SEQ % TQ == 0 and SEQ % TM_PROJ == 0 and CTX_LEN % MXU_DIM == 0
assert FFN_DIM % FFN_CHUNK == 0 and CONV_HALO >= CONV_WIDTH // 2


def _dot(a, b):
    return jnp.dot(a, b, preferred_element_type=F32)


def _dot_nt(a, b):
    return lax.dot_general(a, b, (((1,), (1,)), ((), ())), preferred_element_type=F32)


def _rms(x):
    return x * lax.rsqrt(jnp.mean(x * x, axis=-1, keepdims=True) + EPS)


def _const_spec(shape):
    return pl.BlockSpec(shape, lambda *_: (0,) * len(shape), pipeline_mode=pl.Buffered(1))


def _params(sem):
    return pltpu.CompilerParams(dimension_semantics=sem, vmem_limit_bytes=VMEM_LIMIT_BYTES)


def _mod_kernel(c_ref, w_ref, b_ref, o_ref):
    c = c_ref[...]
    s = c * jax.nn.sigmoid(c)
    w = w_ref[...]
    s_hi = s.astype(BF16)
    s_lo = (s - s_hi.astype(F32)).astype(BF16)
    w_hi = w.astype(BF16)
    w_lo = (w - w_hi.astype(F32)).astype(BF16)
    o_ref[...] = _dot(s_hi, w_hi) + _dot(s_hi, w_lo) + _dot(s_lo, w_hi) + b_ref[...]


def _modulation(cvecs, w_mod, b_mod):
    rows, n_out = cvecs.shape[0], w_mod.shape[1]
    bn = 1024
    return pl.pallas_call(
        _mod_kernel,
        out_shape=jax.ShapeDtypeStruct((rows, n_out), F32),
        grid=(n_out // bn,),
        in_specs=[pl.BlockSpec((rows, D_MODEL), lambda j: (0, 0)),
                  pl.BlockSpec((D_MODEL, bn), lambda j: (0, j)),
                  pl.BlockSpec((1, bn), lambda j: (0, j))],
        out_specs=pl.BlockSpec((rows, bn), lambda j: (0, j)),
        compiler_params=_params(("arbitrary",)),
        name="modulation",
    )(cvecs, w_mod, b_mod)


def _rope(t, cos2, sin2, even):
    swapped = jnp.where(even, pltpu.roll(t, HEAD_DIM - 1, 1), pltpu.roll(t, 1, 1))
    return t * cos2 + swapped * sin2


def _expand_rope_table(row_ref, col_ref):
    rows = row_ref[...]
    per_token = jnp.broadcast_to(rows[:, None, :], (rows.shape[0], GRID_W, HEAD_DIM))
    return per_token.reshape(col_ref.shape) + col_ref[...]


def _inproj_kernel(x_ref, g_ref, sc_ref, sh_ref, w_ref, qg_ref, kg_ref, *refs, full):
    if full:
        (rcos_ref, rsin_ref, ccos_ref, csin_ref,
         q_ref, k_ref, vt_ref, h_ref, sg_ref) = refs
    else:
        k_ref, vt_ref = refs
    tm = x_ref.shape[0]
    h = (_rms(x_ref[...]) * g_ref[...]) * (1.0 + sc_ref[...]) + sh_ref[...]
    hb = h.astype(BF16)
    lane = lax.broadcasted_iota(jnp.int32, (tm, LANES), 1)
    even = (lane % 2) == 0

    if full:
        cos2 = _expand_rope_table(rcos_ref, ccos_ref)
        sin2 = _expand_rope_table(rsin_ref, csin_ref)
        qa = _dot(hb, w_ref[:, 0:Q_DIM])
        scale = HEAD_DIM ** -0.5
        for hd in range(N_HEADS):
            t = _rms(qa[:, hd * HEAD_DIM:(hd + 1) * HEAD_DIM]) * qg_ref[...]
            q_ref[:, hd * HEAD_DIM:(hd + 1) * HEAD_DIM] = (
                _rope(t, cos2, sin2, even) * scale).astype(BF16)

    ka = _dot(hb, w_ref[:, Q_DIM:Q_DIM + KV_DIM])
    ones_col = jnp.where(lane == 0, 1.0, 0.0).astype(BF16)
    for g in range(N_KV_HEADS):
        t = _rms(ka[:, g * HEAD_DIM:(g + 1) * HEAD_DIM]) * kg_ref[...]
        if full:
            t = _rope(t, cos2, sin2, even)
        k_ref[g, :, 0:HEAD_DIM] = t.astype(BF16)
        k_ref[g, :, HEAD_DIM:] = ones_col

    va = _dot(hb, w_ref[:, Q_DIM + KV_DIM:U_OFF])
    vt_ref[...] = va.T.astype(BF16)

    if full:
        ua = _dot(hb, w_ref[:, U_OFF:U_OFF + CONV_DIM])
        ub = _dot(hb, w_ref[:, U_OFF + CONV_DIM:G_OFF])
        h_ref[...] = ua * jax.nn.sigmoid(ub)
        sg_ref[...] = jax.nn.sigmoid(_dot(hb, w_ref[:, G_OFF:IN_DIM]))


def _in_proj(x, g, sc, sh, w_in_b, qg, kg, rope=None, *, tm):
    full = rope is not None
    n = x.shape[0]
    row = lambda i: (i, 0)
    vec = _const_spec((1, D_MODEL))
    hvec = _const_spec((1, HEAD_DIM))
    in_specs = [pl.BlockSpec((tm, D_MODEL), row), vec, vec, vec,
                _const_spec((D_MODEL, IN_DIM)), hvec, hvec]
    operands = [x, g, sc, sh, w_in_b, qg, kg]
    if full:
        grid_rows = tm // GRID_W
        assert grid_rows % SUBLANES_F32 == 0 and rope[2].shape == (tm, HEAD_DIM)
        row_tab = pl.BlockSpec((grid_rows, HEAD_DIM), row)
        in_specs += [row_tab, row_tab, _const_spec((tm, HEAD_DIM)), _const_spec((tm, HEAD_DIM))]
        operands += list(rope)
    k_out = (jax.ShapeDtypeStruct((N_KV_HEADS, n, KP_DIM), BF16),
             pl.BlockSpec((N_KV_HEADS, tm, KP_DIM), lambda i: (0, i, 0)))
    vt_out = (jax.ShapeDtypeStruct((KV_DIM, n), BF16), pl.BlockSpec((KV_DIM, tm), lambda i: (0, i)))
    if full:
        outs = [(jax.ShapeDtypeStruct((n, Q_DIM), BF16), pl.BlockSpec((tm, Q_DIM), row)),
                k_out, vt_out,
                (jax.ShapeDtypeStruct((n, CONV_DIM), F32), pl.BlockSpec((tm, CONV_DIM), row)),
                (jax.ShapeDtypeStruct((n, 2 * D_MODEL), F32), pl.BlockSpec((tm, 2 * D_MODEL), row))]
    else:
        outs = [k_out, vt_out]
    return pl.pallas_call(
        functools.partial(_inproj_kernel, full=full),
        out_shape=[o[0] for o in outs],
        grid=(n // tm,),
        in_specs=in_specs,
        out_specs=[o[1] for o in outs],
        compiler_params=_params(("arbitrary",)),
        name="in_proj" if full else "ctx_proj",
    )(*operands)


def _head_column(ref, hd):
    lane = lax.broadcasted_iota(jnp.int32, ref.shape, 1)
    return jnp.sum(jnp.where(lane == hd, ref[...], 0.0), axis=-1, keepdims=True)


def _shifted_queries(q_ref, shift):
    lane = lax.broadcasted_iota(jnp.int32, (q_ref.shape[0], LANES), 1)
    shift_col = jnp.where(lane == 0, -shift, 0.0).astype(BF16)
    return jnp.concatenate([q_ref[...], shift_col], axis=1)


def _attn_kernel(q_ref, m_ref, kx_ref, vtx_ref, kc_ref, vtc_ref, o_ref):
    qp = _shifted_queries(q_ref, _head_column(m_ref, pl.program_id(0)))

    def chunk(k_blk, vt_blk):
        p = jnp.exp(_dot_nt(k_blk, qp))
        psum = jnp.sum(p.reshape(-1, SUBLANES_F32, TQ), axis=0)
        return _dot(vt_blk, p.astype(BF16)), psum

    acc, psum = chunk(kc_ref[...], vtc_ref[...])
    for c in range(SEQ // TK):
        a, s = chunk(kx_ref[c * TK:(c + 1) * TK, :], vtx_ref[:, c * TK:(c + 1) * TK])
        acc = acc + a
        psum = psum + s
    out_t = acc / jnp.sum(psum, axis=0, keepdims=True)
    o_ref[...] = out_t.T.astype(BF16)


def _kv_specs():
    kv = lambda h, i: (h // GROUP, 0, 0)
    one = pl.Buffered(1)
    return [pl.BlockSpec((None, SEQ, KP_DIM), kv, pipeline_mode=one),
            pl.BlockSpec((None, HEAD_DIM, SEQ), kv, pipeline_mode=one),
            pl.BlockSpec((None, CTX_LEN, KP_DIM), kv, pipeline_mode=one),
            pl.BlockSpec((None, HEAD_DIM, CTX_LEN), kv, pipeline_mode=one)]


def _attention(q, m, kx, vtx, kc, vtc):
    if m.shape[0] == SEQ:
        m_spec = pl.BlockSpec((TQ, LANES), lambda h, i: (i, 0))
    else:
        assert m.shape == (TQ, LANES)
        m_spec = _const_spec((TQ, LANES))
    return pl.pallas_call(
        _attn_kernel,
        out_shape=jax.ShapeDtypeStruct((SEQ, Q_DIM), BF16),
        grid=(N_HEADS, SEQ // TQ),
        in_specs=[pl.BlockSpec((TQ, HEAD_DIM), lambda h, i: (i, h)), m_spec] + _kv_specs(),
        out_specs=pl.BlockSpec((TQ, HEAD_DIM), lambda h, i: (i, h)),
        compiler_params=_params(("arbitrary", "arbitrary")),
        name="attention",
    )(q, m, kx, vtx, kc, vtc)


def _rowmax_kernel(q_ref, kx_ref, kc_ref, o_ref):
    hd = pl.program_id(0)
    qp = _shifted_queries(q_ref, jnp.zeros((q_ref.shape[0], 1), F32))
    m = jnp.max(_dot_nt(qp, kc_ref[...]), axis=-1, keepdims=True)

    def body(c, m):
        k_blk = kx_ref[pl.ds(pl.multiple_of(c * TK, TK), TK), :]
        return jnp.maximum(m, jnp.max(_dot_nt(qp, k_blk), axis=-1, keepdims=True))

    m = lax.fori_loop(0, SEQ // TK, body, m)
    lane = lax.broadcasted_iota(jnp.int32, o_ref.shape[1:], 1)
    o_ref[0] = jnp.where(lane == hd, m, 0.0)


def _row_max(q, kx, kc):
    tq = 256
    kv = lambda h, i: (h // GROUP, 0, 0)
    per_head = pl.pallas_call(
        _rowmax_kernel,
        out_shape=jax.ShapeDtypeStruct((N_HEADS, SEQ, LANES), F32),
        grid=(N_HEADS, SEQ // tq),
        in_specs=[pl.BlockSpec((tq, HEAD_DIM), lambda h, i: (i, h)),
                  pl.BlockSpec((None, SEQ, KP_DIM), kv, pipeline_mode=pl.Buffered(1)),
                  pl.BlockSpec((None, CTX_LEN, KP_DIM), kv, pipeline_mode=pl.Buffered(1))],
        out_specs=pl.BlockSpec((1, tq, LANES), lambda h, i: (h, i, 0)),
        compiler_params=_params(("arbitrary", "arbitrary")),
        name="row_max",
    )(q, kx, kc)
    return jnp.sum(per_head, axis=0)


def _mix_kernel(attn_ref, hm_ref, hp_ref, hn_ref, sg_ref, x_ref, g1_ref,
                wa_ref, wc_ref, wo_ref, dww_ref, dwb_ref, lng_ref, lnb_ref,
                o_ref, hbuf_ref, sbuf_ref):
    i = pl.program_id(0)
    last = pl.num_programs(0) - 1
    tm = x_ref.shape[0]
    hbuf_ref[0:CONV_HALO, :] = jnp.where(i > 0, hp_ref[...], 0.0)
    hbuf_ref[CONV_HALO:CONV_HALO + tm, :] = hm_ref[...]
    hbuf_ref[CONV_HALO + tm:, :] = jnp.where(i < last, hn_ref[...], 0.0)
    span = tm + 2 * CONV_HALO - SUBLANES_F32
    for r in range(1, SUBLANES_F32):
        sbuf_ref[r - 1, 0:span, :] = hbuf_ref[r:r + span, :]

    first = CONV_HALO - CONV_WIDTH // 2
    conv = None
    for k in range(CONV_WIDTH):
        a, r = divmod(first + k, SUBLANES_F32)
        src = hbuf_ref if r == 0 else sbuf_ref.at[r - 1]
        term = src[a * SUBLANES_F32:a * SUBLANES_F32 + tm, :] * dww_ref[k:k + 1, :]
        conv = term + dwb_ref[...] if conv is None else conv + term
    mu = jnp.mean(conv, axis=-1, keepdims=True)
    cen = conv - mu
    var = jnp.mean(cen * cen, axis=-1, keepdims=True)
    ln = cen * lax.rsqrt(var + EPS) * lng_ref[...] + lnb_ref[...]
    act = ln * jax.nn.sigmoid(ln)
    y_b = _dot(act.astype(BF16), wc_ref[...])
    y_a = _dot(attn_ref[...], wa_ref[...])
    merged = sg_ref[:, 0:D_MODEL] * y_a + sg_ref[:, D_MODEL:] * y_b
    o_ref[...] = x_ref[...] + g1_ref[...] * _dot(merged.astype(BF16), wo_ref[...])


def _halo_maps(tm, halo):
    nblk = tm // halo
    n_halo_blocks = SEQ // halo
    prev = lambda i: (jnp.maximum(i * nblk - 1, 0), 0)
    nxt = lambda i: (jnp.minimum((i + 1) * nblk, n_halo_blocks - 1), 0)
    return prev, nxt


def _mix(attn, hglu, sg, x, g1, wa, wc, wo, dww, dwb, lng, lnb):
    tm = TM_MIX
    row = lambda i: (i, 0)
    prev, nxt = _halo_maps(tm, CONV_HALO)
    cvec = _const_spec((1, CONV_DIM))
    ext = tm + 2 * CONV_HALO
    return pl.pallas_call(
        _mix_kernel,
        out_shape=jax.ShapeDtypeStruct((SEQ, D_MODEL), F32),
        grid=(SEQ // tm,),
        in_specs=[pl.BlockSpec((tm, Q_DIM), row),
                  pl.BlockSpec((tm, CONV_DIM), row),
                  pl.BlockSpec((CONV_HALO, CONV_DIM), prev),
                  pl.BlockSpec((CONV_HALO, CONV_DIM), nxt),
                  pl.BlockSpec((tm, 2 * D_MODEL), row),
                  pl.BlockSpec((tm, D_MODEL), row),
                  _const_spec((1, D_MODEL)),
                  _const_spec((Q_DIM, D_MODEL)), _const_spec((CONV_DIM, D_MODEL)),
                  _const_spec((D_MODEL, D_MODEL)),
                  _const_spec((CONV_WIDTH, CONV_DIM)), cvec, cvec, cvec],
        out_specs=pl.BlockSpec((tm, D_MODEL), row),
        scratch_shapes=[pltpu.VMEM((ext, CONV_DIM), F32),
                        pltpu.VMEM((SUBLANES_F32 - 1, ext, CONV_DIM), F32)],
        compiler_params=_params(("arbitrary",)),
        name="mix",
    )(attn, hglu, hglu, hglu, sg, x, g1, wa, wc, wo, dww, dwb, lng, lnb)


def _ffn_kernel(xm_ref, xp_ref, xn_ref, ng_ref, sc_ref, sh_ref, g2_ref, fg_ref,
                wup_ref, dw_ref, wdn_ref, o_ref, xbuf_ref, abuf_ref, act_ref):
    i = pl.program_id(0)
    tm = xm_ref.shape[0]
    xbuf_ref[0:FFN_HALO, :] = xp_ref[...]
    xbuf_ref[FFN_HALO:FFN_HALO + tm, :] = xm_ref[...]
    xbuf_ref[FFN_HALO + tm:, :] = xn_ref[...]
    h = (_rms(xbuf_ref[...]) * ng_ref[...]) * (1.0 + sc_ref[...]) + sh_ref[...]
    hb = h.astype(BF16)
    last = pl.num_programs(0) - 1
    for off in range(0, FFN_DIM, FFN_CHUNK):
        cols = slice(off, off + FFN_CHUNK)
        a = _dot(hb, wup_ref[:, cols])
        abuf_ref[0:FFN_HALO, cols] = jnp.where(i > 0, a[0:FFN_HALO, :], 0.0)
        abuf_ref[FFN_HALO:FFN_HALO + tm, cols] = a[FFN_HALO:FFN_HALO + tm, :]
        abuf_ref[FFN_HALO + tm:, cols] = jnp.where(i < last, a[FFN_HALO + tm:, :], 0.0)
        b = _dot(hb[FFN_HALO:FFN_HALO + tm, :], wup_ref[:, FFN_DIM + off:FFN_DIM + off + FFN_CHUNK])
        conv = (abuf_ref[FFN_HALO - 1:FFN_HALO - 1 + tm, cols] * dw_ref[0:1, cols]
                + abuf_ref[FFN_HALO:FFN_HALO + tm, cols] * dw_ref[1:2, cols]
                + abuf_ref[FFN_HALO + 1:FFN_HALO + 1 + tm, cols] * dw_ref[2:3, cols]
                + dw_ref[3:4, cols])
        act_ref[:, cols] = (jax.nn.gelu(conv, approximate=True) * b).astype(BF16)
    x2 = xm_ref[...] + g2_ref[...] * _dot(act_ref[...], wdn_ref[...])
    o_ref[...] = _rms(x2) * fg_ref[...]


def _ffn(x1, ng, sc, sh, g2, fg, wup, dw, wdn):
    tm = TM_FFN
    row = lambda i: (i, 0)
    prev, nxt = _halo_maps(tm, FFN_HALO)
    vec = _const_spec((1, D_MODEL))
    ext = tm + 2 * FFN_HALO
    return pl.pallas_call(
        _ffn_kernel,
        out_shape=jax.ShapeDtypeStruct((SEQ, D_MODEL), F32),
        grid=(SEQ // tm,),
        in_specs=[pl.BlockSpec((tm, D_MODEL), row),
                  pl.BlockSpec((FFN_HALO, D_MODEL), prev),
                  pl.BlockSpec((FFN_HALO, D_MODEL), nxt),
                  vec, vec, vec, vec, vec,
                  _const_spec((D_MODEL, 2 * FFN_DIM)),
                  _const_spec((SUBLANES_F32, FFN_DIM)),
                  _const_spec((FFN_DIM, D_MODEL))],
        out_specs=pl.BlockSpec((tm, D_MODEL), row),
        scratch_shapes=[pltpu.VMEM((ext, D_MODEL), F32),
                        pltpu.VMEM((ext, FFN_DIM), F32),
                        pltpu.VMEM((tm, FFN_DIM), BF16)],
        compiler_params=_params(("arbitrary",)),
        name="ffn",
    )(x1, x1, x1, ng, sc, sh, g2, fg, wup, dw, wdn)


def _rope_tables(tm):
    half = HEAD_DIM // 2
    rows = SEQ // GRID_W
    inv_freq = ROPE_THETA ** (-jnp.arange(0, half, 2, dtype=F32) / half)
    row_ang = jnp.arange(rows, dtype=F32)[:, None] * inv_freq
    col_ang = jnp.arange(GRID_W, dtype=F32)[:, None] * inv_freq

    def pair_tables(ang):
        c, s = jnp.cos(ang), jnp.sin(ang)
        return jnp.repeat(c, 2, axis=-1), jnp.stack([-s, s], axis=-1).reshape(ang.shape[0], half)

    row_tabs = [jnp.pad(t, ((0, 0), (0, half))) for t in pair_tables(row_ang)]
    col_tabs = [jnp.tile(jnp.pad(t, ((0, 0), (half, 0))), (tm // GRID_W, 1))
                for t in pair_tables(col_ang)]
    return row_tabs + col_tabs


def kernel(x, c, ctx, c_ctx, w_mod, b_mod, norm1_g, w_in, q_norm_g, k_norm_g, w_attn_out,
           conv_dw_w, conv_dw_b, conv_ln_g, conv_ln_b, w_conv_out, w_out, norm2_g,
           w_up, ffn_dw_w, ffn_dw_b, w_down, final_g):
    assert x.shape == (1, SEQ, D_MODEL) and ctx.shape == (1, CTX_LEN, D_MODEL)
    assert w_mod.shape[0] == 1, "single layer"
    x2d = x[0]
    ctx2d = ctx[0]
    r1 = lambda v: v.reshape(1, -1)

    cvecs = jnp.zeros((SUBLANES_F32, D_MODEL), F32).at[0].set(c[0]).at[1].set(c_ctx)
    mod = _modulation(cvecs, w_mod[0], r1(b_mod[0]))
    sh1, sc1, g1, sh2, sc2, g2 = [r1(mod[0, j * D_MODEL:(j + 1) * D_MODEL]) for j in range(6)]
    csh1, csc1 = [r1(mod[1, j * D_MODEL:(j + 1) * D_MODEL]) for j in range(2)]

    w_in_b = w_in[0].astype(BF16)
    ng1, qg, kg = r1(norm1_g[0]), r1(q_norm_g[0]), r1(k_norm_g[0])
    q, kx, vtx, hglu, sg = _in_proj(x2d, ng1, sc1, sh1, w_in_b, qg, kg,
                                    _rope_tables(TM_PROJ), tm=TM_PROJ)
    kc, vtc = _in_proj(ctx2d, ng1, csc1, csh1, w_in_b, qg, kg, tm=CTX_LEN)
    vtx = vtx.reshape(N_KV_HEADS, HEAD_DIM, SEQ)
    vtc = vtc.reshape(N_KV_HEADS, HEAD_DIM, CTX_LEN)

    bound = SHIFT_MARGIN * HEAD_DIM ** 0.5 * jnp.max(jnp.abs(q_norm_g[0])) * jnp.max(jnp.abs(k_norm_g[0]))
    attn = lax.cond(
        bound > SAFE_SHIFT,
        lambda: _attention(q, _row_max(q, kx, kc), kx, vtx, kc, vtc),
        lambda: _attention(q, jnp.full((TQ, LANES), bound, F32), kx, vtx, kc, vtc))

    x1 = _mix(attn, hglu, sg, x2d, g1,
              w_attn_out[0].astype(BF16), w_conv_out[0].astype(BF16), w_out[0].astype(BF16),
              conv_dw_w[0], r1(conv_dw_b[0]), r1(conv_ln_g[0]), r1(conv_ln_b[0]))

    dw = jnp.concatenate([ffn_dw_w[0], ffn_dw_b[0][None, :],
                          jnp.zeros((SUBLANES_F32 - 4, FFN_DIM), F32)], axis=0)
    out = _ffn(x1, r1(norm2_g[0]), sc2, sh2, g2, r1(final_g),
               w_up[0].astype(BF16), dw, w_down[0].astype(BF16))
    return out[None]
```

```python
import functools

import jax
import jax.numpy as jnp
from jax import lax
from jax.experimental import pallas as pl
from jax.experimental.pallas import tpu as pltpu

D_MODEL = 1024
SEQ = 16384
GRID_W = 64
CTX_LEN = 256
N_HEADS = 8
N_KV_HEADS = 2
HEAD_DIM = 128
ROPE_THETA = 10000.0
CONV_DIM = 512
CONV_WIDTH = 31
FFN_DIM = 2816
EPS = 1e-6
Q_DIM = N_HEADS * HEAD_DIM
KV_DIM = N_KV_HEADS * HEAD_DIM
GROUP = N_HEADS // N_KV_HEADS
U_OFF = Q_DIM + 2 * KV_DIM
G_OFF = U_OFF + 2 * CONV_DIM
IN_DIM = G_OFF + 2 * D_MODEL

F32 = jnp.float32
BF16 = jnp.bfloat16

LANES = 128
SUBLANES_F32 = 8
MXU_DIM = 256
VMEM_LIMIT_BYTES = 56 * 1024 * 1024

TM_PROJ = 512
TQ = 2048
TK = 4096
KP_DIM = 2 * HEAD_DIM
TM_MIX = 512
CONV_HALO = 16
TM_FFN = 512
FFN_HALO = SUBLANES_F32
FFN_CHUNK = MXU_DIM
SAFE_SHIFT = 40.0
SHIFT_MARGIN = 1.02

assert SEQ % TK == 0 and SEQ % TQ == 0 and SEQ % TM_PROJ == 0 and CTX_LEN % MXU_DIM == 0
assert FFN_DIM % FFN_CHUNK == 0 and CONV_HALO >= CONV_WIDTH // 2


def _dot(a, b):
    return jnp.dot(a, b, preferred_element_type=F32)


def _dot_nt(a, b):
    return lax.dot_general(a, b, (((1,), (1,)), ((), ())), preferred_element_type=F32)


def _rms(x):
    return x * lax.rsqrt(jnp.mean(x * x, axis=-1, keepdims=True) + EPS)


def _const_spec(shape):
    return pl.BlockSpec(shape, lambda *_: (0,) * len(shape), pipeline_mode=pl.Buffered(1))


def _params(sem):
    return pltpu.CompilerParams(dimension_semantics=sem, vmem_limit_bytes=VMEM_LIMIT_BYTES)


def _mod_kernel(c_ref, w_ref, b_ref, o_ref):
    c = c_ref[...]
    s = c * jax.nn.sigmoid(c)
    w = w_ref[...]
    s_hi = s.astype(BF16)
    s_lo = (s - s_hi.astype(F32)).astype(BF16)
    w_hi = w.astype(BF16)
    w_lo = (w - w_hi.astype(F32)).astype(BF16)
    o_ref[...] = _dot(s_hi, w_hi) + _dot(s_hi, w_lo) + _dot(s_lo, w_hi) + b_ref[...]


def _modulation(cvecs, w_mod, b_mod):
    rows, n_out = cvecs.shape[0], w_mod.shape[1]
    bn = 1024
    return pl.pallas_call(
        _mod_kernel,
        out_shape=jax.ShapeDtypeStruct((rows, n_out), F32),
        grid=(n_out // bn,),
        in_specs=[pl.BlockSpec((rows, D_MODEL), lambda j: (0, 0)),
                  pl.BlockSpec((D_MODEL, bn), lambda j: (0, j)),
                  pl.BlockSpec((1, bn), lambda j: (0, j))],
        out_specs=pl.BlockSpec((rows, bn), lambda j: (0, j)),
        compiler_params=_params(("arbitrary",)),
        name="modulation",
    )(cvecs, w_mod, b_mod)


def _rope(t, cos2, sin2, even):
    swapped = jnp.where(even, pltpu.roll(t, HEAD_DIM - 1, 1), pltpu.roll(t, 1, 1))
    return t * cos2 + swapped * sin2


def _expand_rope_table(row_ref, col_ref):
    rows = row_ref[...]
    per_token = jnp.broadcast_to(rows[:, None, :], (rows.shape[0], GRID_W, HEAD_DIM))
    return per_token.reshape(col_ref.shape) + col_ref[...]


def _inproj_kernel(x_ref, g_ref, sc_ref, sh_ref, w_ref, qg_ref, kg_ref, *refs, full):
    if full:
        (rcos_ref, rsin_ref, ccos_ref, csin_ref,
         q_ref, k_ref, vt_ref, h_ref, sg_ref) = refs
    else:
        k_ref, vt_ref = refs
    tm = x_ref.shape[0]
    h = (_rms(x_ref[...]) * g_ref[...]) * (1.0 + sc_ref[...]) + sh_ref[...]
    hb = h.astype(BF16)
    lane = lax.broadcasted_iota(jnp.int32, (tm, LANES), 1)
    even = (lane % 2) == 0

    if full:
        cos2 = _expand_rope_table(rcos_ref, ccos_ref)
        sin2 = _expand_rope_table(rsin_ref, csin_ref)
        qa = _dot(hb, w_ref[:, 0:Q_DIM])
        scale = HEAD_DIM ** -0.5
        for hd in range(N_HEADS):
            t = _rms(qa[:, hd * HEAD_DIM:(hd + 1) * HEAD_DIM]) * qg_ref[...]
            q_ref[:, hd * HEAD_DIM:(hd + 1) * HEAD_DIM] = (
                _rope(t, cos2, sin2, even) * scale).astype(BF16)

    ka = _dot(hb, w_ref[:, Q_DIM:Q_DIM + KV_DIM])
    ones_col = jnp.where(lane == 0, 1.0, 0.0).astype(BF16)
    for g in range(N_KV_HEADS):
        t = _rms(ka[:, g * HEAD_DIM:(g + 1) * HEAD_DIM]) * kg_ref[...]
        if full:
            t = _rope(t, cos2, sin2, even)
        k_ref[g, :, 0:HEAD_DIM] = t.astype(BF16)
        k_ref[g, :, HEAD_DIM:] = ones_col

    va = _dot(hb, w_ref[:, Q_DIM + KV_DIM:U_OFF])
    vt_ref[...] = va.T.astype(BF16)

    if full:
        ua = _dot(hb, w_ref[:, U_OFF:U_OFF + CONV_DIM])
        ub = _dot(hb, w_ref[:, U_OFF + CONV_DIM:G_OFF])
        h_ref[...] = ua * jax.nn.sigmoid(ub)
        sg_ref[...] = jax.nn.sigmoid(_dot(hb, w_ref[:, G_OFF:IN_DIM]))


def _in_proj(x, g, sc, sh, w_in_b, qg, kg, rope=None, *, tm):
    full = rope is not None
    n = x.shape[0]
    row = lambda i: (i, 0)
    vec = _const_spec((1, D_MODEL))
    hvec = _const_spec((1, HEAD_DIM))
    in_specs = [pl.BlockSpec((tm, D_MODEL), row), vec, vec, vec,
                _const_spec((D_MODEL, IN_DIM)), hvec, hvec]
    operands = [x, g, sc, sh, w_in_b, qg, kg]
    if full:
        grid_rows = tm // GRID_W
        assert grid_rows % SUBLANES_F32 == 0 and rope[2].shape == (tm, HEAD_DIM)
        row_tab = pl.BlockSpec((grid_rows, HEAD_DIM), row)
        in_specs += [row_tab, row_tab, _const_spec((tm, HEAD_DIM)), _const_spec((tm, HEAD_DIM))]
        operands += list(rope)
    k_out = (jax.ShapeDtypeStruct((N_KV_HEADS, n, KP_DIM), BF16),
             pl.BlockSpec((N_KV_HEADS, tm, KP_DIM), lambda i: (0, i, 0)))
    vt_out = (jax.ShapeDtypeStruct((KV_DIM, n), BF16), pl.BlockSpec((KV_DIM, tm), lambda i: (0, i)))
    if full:
        outs = [(jax.ShapeDtypeStruct((n, Q_DIM), BF16), pl.BlockSpec((tm, Q_DIM), row)),
                k_out, vt_out,
                (jax.ShapeDtypeStruct((n, CONV_DIM), F32), pl.BlockSpec((tm, CONV_DIM), row)),
                (jax.ShapeDtypeStruct((n, 2 * D_MODEL), F32), pl.BlockSpec((tm, 2 * D_MODEL), row))]
    else:
        outs = [k_out, vt_out]
    return pl.pallas_call(
        functools.partial(_inproj_kernel, full=full),
        out_shape=[o[0] for o in outs],
        grid=(n // tm,),
        in_specs=in_specs,
        out_specs=[o[1] for o in outs],
        compiler_params=_params(("arbitrary",)),
        name="in_proj" if full else "ctx_proj",
    )(*operands)


def _head_column(ref, hd):
    lane = lax.broadcasted_iota(jnp.int32, ref.shape, 1)
    return jnp.sum(jnp.where(lane == hd, ref[...], 0.0), axis=-1, keepdims=True)


def _shifted_queries(q_ref, shift):
    lane = lax.broadcasted_iota(jnp.int32, (q_ref.shape[0], LANES), 1)
    shift_col = jnp.where(lane == 0, -shift, 0.0).astype(BF16)
    return jnp.concatenate([q_ref[...], shift_col], axis=1)


def _attn_kernel(q_ref, m_ref, kx_ref, vtx_ref, kc_ref, vtc_ref, o_ref):
    qp = _shifted_queries(q_ref, _head_column(m_ref, pl.program_id(0)))

    def chunk(k_blk, vt_blk):
        p = jnp.exp(_dot_nt(k_blk, qp))
        psum = jnp.sum(p.reshape(-1, SUBLANES_F32, TQ), axis=0)
        return _dot(vt_blk, p.astype(BF16)), psum

    acc, psum = chunk(kc_ref[...], vtc_ref[...])
    for c in range(SEQ // TK):
        a, s = chunk(kx_ref[c * TK:(c + 1) * TK, :], vtx_ref[:, c * TK:(c + 1) * TK])
        acc = acc + a
        psum = psum + s
    out_t = acc / jnp.sum(psum, axis=0, keepdims=True)
    o_ref[...] = out_t.T.astype(BF16)


def _kv_specs():
    kv = lambda h, i: (h // GROUP, 0, 0)
    one = pl.Buffered(1)
    return [pl.BlockSpec((None, SEQ, KP_DIM), kv, pipeline_mode=one),
            pl.BlockSpec((None, HEAD_DIM, SEQ), kv, pipeline_mode=one),
            pl.BlockSpec((None, CTX_LEN, KP_DIM), kv, pipeline_mode=one),
            pl.BlockSpec((None, HEAD_DIM, CTX_LEN), kv, pipeline_mode=one)]


def _attention(q, m, kx, vtx, kc, vtc):
    return pl.pallas_call(
        _attn_kernel,
        out_shape=jax.ShapeDtypeStruct((SEQ, Q_DIM), BF16),
        grid=(N_HEADS, SEQ // TQ),
        in_specs=[pl.BlockSpec((TQ, HEAD_DIM), lambda h, i: (i, h)),
                  pl.BlockSpec((TQ, LANES), lambda h, i: (i, 0))] + _kv_specs(),
        out_specs=pl.BlockSpec((TQ, HEAD_DIM), lambda h, i: (i, h)),
        compiler_params=_params(("arbitrary", "arbitrary")),
        name="attention",
    )(q, m, kx, vtx, kc, vtc)


def _rowmax_kernel(q_ref, kx_ref, kc_ref, o_ref):
    hd = pl.program_id(0)
    qp = _shifted_queries(q_ref, jnp.zeros((q_ref.shape[0], 1), F32))
    m = jnp.max(_dot_nt(qp, kc_ref[...]), axis=-1, keepdims=True)

    def body(c, m):
        k_blk = kx_ref[pl.ds(pl.multiple_of(c * TK, TK), TK), :]
        return jnp.maximum(m, jnp.max(_dot_nt(qp, k_blk), axis=-1, keepdims=True))

    m = lax.fori_loop(0, SEQ // TK, body, m)
    lane = lax.broadcasted_iota(jnp.int32, o_ref.shape[1:], 1)
    o_ref[0] = jnp.where(lane == hd, m, 0.0)


def _row_max(q, kx, kc):
    tq = 256
    kv = lambda h, i: (h // GROUP, 0, 0)
    per_head = pl.pallas_call(
        _rowmax_kernel,
        out_shape=jax.ShapeDtypeStruct((N_HEADS, SEQ, LANES), F32),
        grid=(N_HEADS, SEQ // tq),
        in_specs=[pl.BlockSpec((tq, HEAD_DIM), lambda h, i: (i, h)),
                  pl.BlockSpec((None, SEQ, KP_DIM), kv, pipeline_mode=pl.Buffered(1)),
                  pl.BlockSpec((None, CTX_LEN, KP_DIM), kv, pipeline_mode=pl.Buffered(1))],
        out_specs=pl.BlockSpec((1, tq, LANES), lambda h, i: (h, i, 0)),
        compiler_params=_params(("arbitrary", "arbitrary")),
        name="row_max",
    )(q, kx, kc)
    return jnp.sum(per_head, axis=0)


def _mix_kernel(attn_ref, hm_ref, hp_ref, hn_ref, sg_ref, x_ref, g1_ref,
                wa_ref, wc_ref, wo_ref, dww_ref, dwb_ref, lng_ref, lnb_ref,
                o_ref, hbuf_ref, sbuf_ref):
    i = pl.program_id(0)
    last = pl.num_programs(0) - 1
    tm = x_ref.shape[0]
    hbuf_ref[0:CONV_HALO, :] = jnp.where(i > 0, hp_ref[...], 0.0)
    hbuf_ref[CONV_HALO:CONV_HALO + tm, :] = hm_ref[...]
    hbuf_ref[CONV_HALO + tm:, :] = jnp.where(i < last, hn_ref[...], 0.0)
    span = tm + 2 * CONV_HALO - SUBLANES_F32
    for r in range(1, SUBLANES_F32):
        sbuf_ref[r - 1, 0:span, :] = hbuf_ref[r:r + span, :]

    first = CONV_HALO - CONV_WIDTH // 2
    conv = None
    for k in range(CONV_WIDTH):
        a, r = divmod(first + k, SUBLANES_F32)
        src = hbuf_ref if r == 0 else sbuf_ref.at[r - 1]
        term = src[a * SUBLANES_F32:a * SUBLANES_F32 + tm, :] * dww_ref[k:k + 1, :]
        conv = term + dwb_ref[...] if conv is None else conv + term
    mu = jnp.mean(conv, axis=-1, keepdims=True)
    cen = conv - mu
    var = jnp.mean(cen * cen, axis=-1, keepdims=True)
    ln = cen * lax.rsqrt(var + EPS) * lng_ref[...] + lnb_ref[...]
    act = ln * jax.nn.sigmoid(ln)
    y_b = _dot(act.astype(BF16), wc_ref[...])
    y_a = _dot(attn_ref[...], wa_ref[...])
    merged = sg_ref[:, 0:D_MODEL] * y_a + sg_ref[:, D_MODEL:] * y_b
    o_ref[...] = x_ref[...] + g1_ref[...] * _dot(merged.astype(BF16), wo_ref[...])


def _halo_maps(tm, halo):
    nblk = tm // halo
    n_halo_blocks = SEQ // halo
    prev = lambda i: (jnp.maximum(i * nblk - 1, 0), 0)
    nxt = lambda i: (jnp.minimum((i + 1) * nblk, n_halo_blocks - 1), 0)
    return prev, nxt


def _mix(attn, hglu, sg, x, g1, wa, wc, wo, dww, dwb, lng, lnb):
    tm = TM_MIX
    row = lambda i: (i, 0)
    prev, nxt = _halo_maps(tm, CONV_HALO)
    cvec = _const_spec((1, CONV_DIM))
    ext = tm + 2 * CONV_HALO
    return pl.pallas_call(
        _mix_kernel,
        out_shape=jax.ShapeDtypeStruct((SEQ, D_MODEL), F32),
        grid=(SEQ // tm,),
        in_specs=[pl.BlockSpec((tm, Q_DIM), row),
                  pl.BlockSpec((tm, CONV_DIM), row),
                  pl.BlockSpec((CONV_HALO, CONV_DIM), prev),
                  pl.BlockSpec((CONV_HALO, CONV_DIM), nxt),
                  pl.BlockSpec((tm, 2 * D_MODEL), row),
                  pl.BlockSpec((tm, D_MODEL), row),
                  _const_spec((1, D_MODEL)),
                  _const_spec((Q_DIM, D_MODEL)), _const_spec((CONV_DIM, D_MODEL)),
                  _const_spec((D_MODEL, D_MODEL)),
                  _const_spec((CONV_WIDTH, CONV_DIM)), cvec, cvec, cvec],
        out_specs=pl.BlockSpec((tm, D_MODEL), row),
        scratch_shapes=[pltpu.VMEM((ext, CONV_DIM), F32),
                        pltpu.VMEM((SUBLANES_F32 - 1, ext, CONV_DIM), F32)],
        compiler_params=_params(("arbitrary",)),
        name="mix",
    )(attn, hglu, hglu, hglu, sg, x, g1, wa, wc, wo, dww, dwb, lng, lnb)


def _ffn_kernel(xm_ref, xp_ref, xn_ref, ng_ref, sc_ref, sh_ref, g2_ref, fg_ref,
                wup_ref, dw_ref, wdn_ref, o_ref, xbuf_ref, abuf_ref, act_ref):
    i = pl.program_id(0)
    tm = xm_ref.shape[0]
    xbuf_ref[0:FFN_HALO, :] = xp_ref[...]
    xbuf_ref[FFN_HALO:FFN_HALO + tm, :] = xm_ref[...]
    xbuf_ref[FFN_HALO + tm:, :] = xn_ref[...]
    h = (_rms(xbuf_ref[...]) * ng_ref[...]) * (1.0 + sc_ref[...]) + sh_ref[...]
    hb = h.astype(BF16)
    last = pl.num_programs(0) - 1
    for off in range(0, FFN_DIM, FFN_CHUNK):
        cols = slice(off, off + FFN_CHUNK)
        a = _dot(hb, wup_ref[:, cols])
        abuf_ref[0:FFN_HALO, cols] = jnp.where(i > 0, a[0:FFN_HALO, :], 0.0)
        abuf_ref[FFN_HALO:FFN_HALO + tm, cols] = a[FFN_HALO:FFN_HALO + tm, :]
        abuf_ref[FFN_HALO + tm:, cols] = jnp.where(i < last, a[FFN_HALO + tm:, :], 0.0)
        b = _dot(hb[FFN_HALO:FFN_HALO + tm, :], wup_ref[:, FFN_DIM + off:FFN_DIM + off + FFN_CHUNK])
        conv = (abuf_ref[FFN_HALO - 1:FFN_HALO - 1 + tm, cols] * dw_ref[0:1, cols]
                + abuf_ref[FFN_HALO:FFN_HALO + tm, cols] * dw_ref[1:2, cols]
                + abuf_ref[FFN_HALO + 1:FFN_HALO + 1 + tm, cols] * dw_ref[2:3, cols]
                + dw_ref[3:4, cols])
        act_ref[:, cols] = (jax.nn.gelu(conv, approximate=True) * b).astype(BF16)
    x2 = xm_ref[...] + g2_ref[...] * _dot(act_ref[...], wdn_ref[...])
    o_ref[...] = _rms(x2) * fg_ref[...]


def _ffn(x1, ng, sc, sh, g2, fg, wup, dw, wdn):
    tm = TM_FFN
    row = lambda i: (i, 0)
    prev, nxt = _halo_maps(tm, FFN_HALO)
    vec = _const_spec((1, D_MODEL))
    ext = tm + 2 * FFN_HALO
    return pl.pallas_call(
        _ffn_kernel,
        out_shape=jax.ShapeDtypeStruct((SEQ, D_MODEL), F32),
        grid=(SEQ // tm,),
        in_specs=[pl.BlockSpec((tm, D_MODEL), row),
                  pl.BlockSpec((FFN_HALO, D_MODEL), prev),
                  pl.BlockSpec((FFN_HALO, D_MODEL), nxt),
                  vec, vec, vec, vec, vec,
                  _const_spec((D_MODEL, 2 * FFN_DIM)),
                  _const_spec((SUBLANES_F32, FFN_DIM)),
                  _const_spec((FFN_DIM, D_MODEL))],
        out_specs=pl.BlockSpec((tm, D_MODEL), row),
        scratch_shapes=[pltpu.VMEM((ext, D_MODEL), F32),
                        pltpu.VMEM((ext, FFN_DIM), F32),
                        pltpu.VMEM((tm, FFN_DIM), BF16)],
        compiler_params=_params(("arbitrary",)),
        name="ffn",
    )(x1, x1, x1, ng, sc, sh, g2, fg, wup, dw, wdn)


def _rope_tables(tm):
    half = HEAD_DIM // 2
    rows = SEQ // GRID_W
    inv_freq = ROPE_THETA ** (-jnp.arange(0, half, 2, dtype=F32) / half)
    row_ang = jnp.arange(rows, dtype=F32)[:, None] * inv_freq
    col_ang = jnp.arange(GRID_W, dtype=F32)[:, None] * inv_freq

    def pair_tables(ang):
        c, s = jnp.cos(ang), jnp.sin(ang)
        return jnp.repeat(c, 2, axis=-1), jnp.stack([-s, s], axis=-1).reshape(ang.shape[0], half)

    row_tabs = [jnp.pad(t, ((0, 0), (0, half))) for t in pair_tables(row_ang)]
    col_tabs = [jnp.tile(jnp.pad(t, ((0, 0), (half, 0))), (tm // GRID_W, 1))
                for t in pair_tables(col_ang)]
    return row_tabs + col_tabs


def kernel(x, c, ctx, c_ctx, w_mod, b_mod, norm1_g, w_in, q_norm_g, k_norm_g, w_attn_out,
           conv_dw_w, conv_dw_b, conv_ln_g, conv_ln_b, w_conv_out, w_out, norm2_g,
           w_up, ffn_dw_w, ffn_dw_b, w_down, final_g):
    assert x.shape == (1, SEQ, D_MODEL) and ctx.shape == (1, CTX_LEN, D_MODEL)
    assert w_mod.shape[0] == 1, "single layer"
    x2d = x[0]
    ctx2d = ctx[0]
    r1 = lambda v: v.reshape(1, -1)

    cvecs = jnp.zeros((SUBLANES_F32, D_MODEL), F32).at[0].set(c[0]).at[1].set(c_ctx)
    mod = _modulation(cvecs, w_mod[0], r1(b_mod[0]))
    sh1, sc1, g1, sh2, sc2, g2 = [r1(mod[0, j * D_MODEL:(j + 1) * D_MODEL]) for j in range(6)]
    csh1, csc1 = [r1(mod[1, j * D_MODEL:(j + 1) * D_MODEL]) for j in range(2)]

    w_in_b = w_in[0].astype(BF16)
    ng1, qg, kg = r1(norm1_g[0]), r1(q_norm_g[0]), r1(k_norm_g[0])
    q, kx, vtx, hglu, sg = _in_proj(x2d, ng1, sc1, sh1, w_in_b, qg, kg,
                                    _rope_tables(TM_PROJ), tm=TM_PROJ)
    kc, vtc = _in_proj(ctx2d, ng1, csc1, csh1, w_in_b, qg, kg, tm=CTX_LEN)
    vtx = vtx.reshape(N_KV_HEADS, HEAD_DIM, SEQ)
    vtc = vtc.reshape(N_KV_HEADS, HEAD_DIM, CTX_LEN)

    bound = SHIFT_MARGIN * HEAD_DIM ** 0.5 * jnp.max(jnp.abs(q_norm_g[0])) * jnp.max(jnp.abs(k_norm_g[0]))
    shifts = lax.cond(bound > SAFE_SHIFT,
                      lambda: _row_max(q, kx, kc),
                      lambda: jnp.full((SEQ, LANES), bound, F32))
    attn = _attention(q, shifts, kx, vtx, kc, vtc)

    x1 = _mix(attn, hglu, sg, x2d, g1,
              w_attn_out[0].astype(BF16), w_conv_out[0].astype(BF16), w_out[0].astype(BF16),
              conv_dw_w[0], r1(conv_dw_b[0]), r1(conv_ln_g[0]), r1(conv_ln_b[0]))

    dw = jnp.concatenate([ffn_dw_w[0], ffn_dw_b[0][None, :],
                          jnp.zeros((SUBLANES_F32 - 4, FFN_DIM), F32)], axis=0)
    out = _ffn(x1, r1(norm2_g[0]), sc2, sh2, g2, r1(final_g),
               w_up[0].astype(BF16), dw, w_down[0].astype(BF16))
    return out[None]
```

```python
import functools

import jax
import jax.numpy as jnp
from jax import lax
from jax.experimental import pallas as pl
from jax.experimental.pallas import tpu as pltpu

D_MODEL = 1024
SEQ = 16384
GRID_W = 64
CTX_LEN = 256
N_HEADS = 8
N_KV_HEADS = 2
HEAD_DIM = 128
ROPE_THETA = 10000.0
CONV_DIM = 512
CONV_WIDTH = 31
FFN_DIM = 2816
EPS = 1e-6
Q_DIM = N_HEADS * HEAD_DIM
KV_DIM = N_KV_HEADS * HEAD_DIM
GROUP = N_HEADS // N_KV_HEADS
U_OFF = Q_DIM + 2 * KV_DIM
G_OFF = U_OFF + 2 * CONV_DIM
IN_DIM = G_OFF + 2 * D_MODEL

F32 = jnp.float32
BF16 = jnp.bfloat16

LANES = 128
SUBLANES_F32 = 8
MXU_DIM = 256
VMEM_LIMIT_BYTES = 56 * 1024 * 1024

TM_PROJ = 512
TQ = 2048
TK = 4096
KP_DIM = 2 * HEAD_DIM
TM_MIX = 512
CONV_HALO = 16
TM_FFN = 512
FFN_HALO = SUBLANES_F32
FFN_CHUNK = MXU_DIM
SAFE_SHIFT = 40.0
SHIFT_MARGIN = 1.02

assert SEQ % TK == 0 and SEQ % TQ == 0 and SEQ % TM_PROJ == 0 and CTX_LEN % MXU_DIM == 0
assert FFN_DIM % FFN_CHUNK == 0 and CONV_HALO >= CONV_WIDTH // 2


def _dot(a, b):
    return jnp.dot(a, b, preferred_element_type=F32)


def _dot_nt(a, b):
    return lax.dot_general(a, b, (((1,), (1,)), ((), ())), preferred_element_type=F32)


def _rms(x):
    return x * lax.rsqrt(jnp.mean(x * x, axis=-1, keepdims=True) + EPS)


def _const_spec(shape):
    return pl.BlockSpec(shape, lambda *_: (0,) * len(shape), pipeline_mode=pl.Buffered(1))


def _params(sem):
    return pltpu.CompilerParams(dimension_semantics=sem, vmem_limit_bytes=VMEM_LIMIT_BYTES)


def _mod_kernel(c_ref, w_ref, b_ref, o_ref):
    c = c_ref[...]
    s = c * jax.nn.sigmoid(c)
    w = w_ref[...]
    s_hi = s.astype(BF16)
    s_lo = (s - s_hi.astype(F32)).astype(BF16)
    w_hi = w.astype(BF16)
    w_lo = (w - w_hi.astype(F32)).astype(BF16)
    o_ref[...] = _dot(s_hi, w_hi) + _dot(s_hi, w_lo) + _dot(s_lo, w_hi) + b_ref[...]


def _modulation(cvecs, w_mod, b_mod):
    rows, n_out = cvecs.shape[0], w_mod.shape[1]
    bn = 1024
    return pl.pallas_call(
        _mod_kernel,
        out_shape=jax.ShapeDtypeStruct((rows, n_out), F32),
        grid=(n_out // bn,),
        in_specs=[pl.BlockSpec((rows, D_MODEL), lambda j: (0, 0)),
                  pl.BlockSpec((D_MODEL, bn), lambda j: (0, j)),
                  pl.BlockSpec((1, bn), lambda j: (0, j))],
        out_specs=pl.BlockSpec((rows, bn), lambda j: (0, j)),
        compiler_params=_params(("arbitrary",)),
        name="modulation",
    )(cvecs, w_mod, b_mod)


def _rope(t, cos2, sin2, even):
    swapped = jnp.where(even, pltpu.roll(t, HEAD_DIM - 1, 1), pltpu.roll(t, 1, 1))
    return t * cos2 + swapped * sin2


def _expand_rope_table(row_ref, col_ref):
    rows = row_ref[...]
    per_token = jnp.broadcast_to(rows[:, None, :], (rows.shape[0], GRID_W, HEAD_DIM))
    return per_token.reshape(col_ref.shape) + col_ref[...]


def _inproj_kernel(x_ref, g_ref, sc_ref, sh_ref, w_ref, qg_ref, kg_ref, *refs, full):
    if full:
        (rcos_ref, rsin_ref, ccos_ref, csin_ref,
         q_ref, k_ref, vt_ref, h_ref, sg_ref) = refs
    else:
        k_ref, vt_ref = refs
    tm = x_ref.shape[0]
    h = (_rms(x_ref[...]) * g_ref[...]) * (1.0 + sc_ref[...]) + sh_ref[...]
    hb = h.astype(BF16)
    lane = lax.broadcasted_iota(jnp.int32, (tm, LANES), 1)
    even = (lane % 2) == 0

    if full:
        cos2 = _expand_rope_table(rcos_ref, ccos_ref)
        sin2 = _expand_rope_table(rsin_ref, csin_ref)
        qa = _dot(hb, w_ref[:, 0:Q_DIM])
        scale = HEAD_DIM ** -0.5
        for hd in range(N_HEADS):
            t = _rms(qa[:, hd * HEAD_DIM:(hd + 1) * HEAD_DIM]) * qg_ref[...]
            q_ref[:, hd * HEAD_DIM:(hd + 1) * HEAD_DIM] = (
                _rope(t, cos2, sin2, even) * scale).astype(BF16)

    ka = _dot(hb, w_ref[:, Q_DIM:Q_DIM + KV_DIM])
    ones_col = jnp.where(lane == 0, 1.0, 0.0).astype(BF16)
    for g in range(N_KV_HEADS):
        t = _rms(ka[:, g * HEAD_DIM:(g + 1) * HEAD_DIM]) * kg_ref[...]
        if full:
            t = _rope(t, cos2, sin2, even)
        k_ref[g, :, 0:HEAD_DIM] = t.astype(BF16)
        k_ref[g, :, HEAD_DIM:] = ones_col

    va = _dot(hb, w_ref[:, Q_DIM + KV_DIM:U_OFF])
    vt_ref[...] = va.T.astype(BF16)

    if full:
        ua = _dot(hb, w_ref[:, U_OFF:U_OFF + CONV_DIM])
        ub = _dot(hb, w_ref[:, U_OFF + CONV_DIM:G_OFF])
        h_ref[...] = ua * jax.nn.sigmoid(ub)
        sg_ref[...] = jax.nn.sigmoid(_dot(hb, w_ref[:, G_OFF:IN_DIM])).astype(BF16)


def _in_proj(x, g, sc, sh, w_in_b, qg, kg, rope=None, *, tm):
    full = rope is not None
    n = x.shape[0]
    row = lambda i: (i, 0)
    vec = _const_spec((1, D_MODEL))
    hvec = _const_spec((1, HEAD_DIM))
    in_specs = [pl.BlockSpec((tm, D_MODEL), row), vec, vec, vec,
                _const_spec((D_MODEL, IN_DIM)), hvec, hvec]
    operands = [x, g, sc, sh, w_in_b, qg, kg]
    if full:
        grid_rows = tm // GRID_W
        assert grid_rows % SUBLANES_F32 == 0 and rope[2].shape == (tm, HEAD_DIM)
        row_tab = pl.BlockSpec((grid_rows, HEAD_DIM), row)
        in_specs += [row_tab, row_tab, _const_spec((tm, HEAD_DIM)), _const_spec((tm, HEAD_DIM))]
        operands += list(rope)
    k_out = (jax.ShapeDtypeStruct((N_KV_HEADS, n, KP_DIM), BF16),
             pl.BlockSpec((N_KV_HEADS, tm, KP_DIM), lambda i: (0, i, 0)))
    vt_out = (jax.ShapeDtypeStruct((KV_DIM, n), BF16), pl.BlockSpec((KV_DIM, tm), lambda i: (0, i)))
    if full:
        outs = [(jax.ShapeDtypeStruct((n, Q_DIM), BF16), pl.BlockSpec((tm, Q_DIM), row)),
                k_out, vt_out,
                (jax.ShapeDtypeStruct((n, CONV_DIM), F32), pl.BlockSpec((tm, CONV_DIM), row)),
                (jax.ShapeDtypeStruct((n, 2 * D_MODEL), BF16), pl.BlockSpec((tm, 2 * D_MODEL), row))]
    else:
        outs = [k_out, vt_out]
    return pl.pallas_call(
        functools.partial(_inproj_kernel, full=full),
        out_shape=[o[0] for o in outs],
        grid=(n // tm,),
        in_specs=in_specs,
        out_specs=[o[1] for o in outs],
        compiler_params=_params(("arbitrary",)),
        name="in_proj" if full else "ctx_proj",
    )(*operands)


def _head_column(ref, hd):
    lane = lax.broadcasted_iota(jnp.int32, ref.shape, 1)
    return jnp.sum(jnp.where(lane == hd, ref[...], 0.0), axis=-1, keepdims=True)


def _shifted_queries(q_ref, shift):
    lane = lax.broadcasted_iota(jnp.int32, (q_ref.shape[0], LANES), 1)
    shift_col = jnp.where(lane == 0, -shift, 0.0).astype(BF16)
    return jnp.concatenate([q_ref[...], shift_col], axis=1)


def _attn_kernel(q_ref, m_ref, kx_ref, vtx_ref, kc_ref, vtc_ref, o_ref):
    qp = _shifted_queries(q_ref, _head_column(m_ref, pl.program_id(0)))

    def chunk(k_blk, vt_blk):
        p = jnp.exp(_dot_nt(k_blk, qp))
        psum = jnp.sum(p.reshape(-1, SUBLANES_F32, TQ), axis=0)
        return _dot(vt_blk, p.astype(BF16)), psum

    acc, psum = chunk(kc_ref[...], vtc_ref[...])
    for c in range(SEQ // TK):
        a, s = chunk(kx_ref[c * TK:(c + 1) * TK, :], vtx_ref[:, c * TK:(c + 1) * TK])
        acc = acc + a
        psum = psum + s
    out_t = acc / jnp.sum(psum, axis=0, keepdims=True)
    o_ref[...] = out_t.T.astype(BF16)


def _kv_specs():
    kv = lambda h, i: (h // GROUP, 0, 0)
    one = pl.Buffered(1)
    return [pl.BlockSpec((None, SEQ, KP_DIM), kv, pipeline_mode=one),
            pl.BlockSpec((None, HEAD_DIM, SEQ), kv, pipeline_mode=one),
            pl.BlockSpec((None, CTX_LEN, KP_DIM), kv, pipeline_mode=one),
            pl.BlockSpec((None, HEAD_DIM, CTX_LEN), kv, pipeline_mode=one)]


def _attention(q, m, kx, vtx, kc, vtc):
    return pl.pallas_call(
        _attn_kernel,
        out_shape=jax.ShapeDtypeStruct((SEQ, Q_DIM), BF16),
        grid=(N_HEADS, SEQ // TQ),
        in_specs=[pl.BlockSpec((TQ, HEAD_DIM), lambda h, i: (i, h)),
                  pl.BlockSpec((TQ, LANES), lambda h, i: (i, 0))] + _kv_specs(),
        out_specs=pl.BlockSpec((TQ, HEAD_DIM), lambda h, i: (i, h)),
        compiler_params=_params(("arbitrary", "arbitrary")),
        name="attention",
    )(q, m, kx, vtx, kc, vtc)


def _rowmax_kernel(q_ref, kx_ref, kc_ref, o_ref):
    hd = pl.program_id(0)
    qp = _shifted_queries(q_ref, jnp.zeros((q_ref.shape[0], 1), F32))
    m = jnp.max(_dot_nt(qp, kc_ref[...]), axis=-1, keepdims=True)

    def body(c, m):
        k_blk = kx_ref[pl.ds(pl.multiple_of(c * TK, TK), TK), :]
        return jnp.maximum(m, jnp.max(_dot_nt(qp, k_blk), axis=-1, keepdims=True))

    m = lax.fori_loop(0, SEQ // TK, body, m)
    lane = lax.broadcasted_iota(jnp.int32, o_ref.shape[1:], 1)
    o_ref[0] = jnp.where(lane == hd, m, 0.0)


def _row_max(q, kx, kc):
    tq = 256
    kv = lambda h, i: (h // GROUP, 0, 0)
    per_head = pl.pallas_call(
        _rowmax_kernel,
        out_shape=jax.ShapeDtypeStruct((N_HEADS, SEQ, LANES), F32),
        grid=(N_HEADS, SEQ // tq),
        in_specs=[pl.BlockSpec((tq, HEAD_DIM), lambda h, i: (i, h)),
                  pl.BlockSpec((None, SEQ, KP_DIM), kv, pipeline_mode=pl.Buffered(1)),
                  pl.BlockSpec((None, CTX_LEN, KP_DIM), kv, pipeline_mode=pl.Buffered(1))],
        out_specs=pl.BlockSpec((1, tq, LANES), lambda h, i: (h, i, 0)),
        compiler_params=_params(("arbitrary", "arbitrary")),
        name="row_max",
    )(q, kx, kc)
    return jnp.sum(per_head, axis=0)


def _mix_kernel(attn_ref, hm_ref, hp_ref, hn_ref, sg_ref, x_ref, g1_ref,
                wa_ref, wc_ref, wo_ref, dww_ref, dwb_ref, lng_ref, lnb_ref,
                o_ref, hbuf_ref, sbuf_ref):
    i = pl.program_id(0)
    last = pl.num_programs(0) - 1
    tm = x_ref.shape[0]
    hbuf_ref[0:CONV_HALO, :] = jnp.where(i > 0, hp_ref[...], 0.0)
    hbuf_ref[CONV_HALO:CONV_HALO + tm, :] = hm_ref[...]
    hbuf_ref[CONV_HALO + tm:, :] = jnp.where(i < last, hn_ref[...], 0.0)
    span = tm + 2 * CONV_HALO - SUBLANES_F32
    for r in range(1, SUBLANES_F32):
        sbuf_ref[r - 1, 0:span, :] = hbuf_ref[r:r + span, :]

    first = CONV_HALO - CONV_WIDTH // 2
    conv = None
    for k in range(CONV_WIDTH):
        a, r = divmod(first + k, SUBLANES_F32)
        src = hbuf_ref if r == 0 else sbuf_ref.at[r - 1]
        term = src[a * SUBLANES_F32:a * SUBLANES_F32 + tm, :] * dww_ref[k:k + 1, :]
        conv = term + dwb_ref[...] if conv is None else conv + term
    mu = jnp.mean(conv, axis=-1, keepdims=True)
    cen = conv - mu
    var = jnp.mean(cen * cen, axis=-1, keepdims=True)
    ln = cen * lax.rsqrt(var + EPS) * lng_ref[...] + lnb_ref[...]
    act = ln * jax.nn.sigmoid(ln)
    y_b = _dot(act.astype(BF16), wc_ref[...])
    y_a = _dot(attn_ref[...], wa_ref[...])
    merged = sg_ref[:, 0:D_MODEL] * y_a + sg_ref[:, D_MODEL:] * y_b
    o_ref[...] = x_ref[...] + g1_ref[...] * _dot(merged.astype(BF16), wo_ref[...])


def _halo_maps(tm, halo):
    nblk = tm // halo
    n_halo_blocks = SEQ // halo
    prev = lambda i: (jnp.maximum(i * nblk - 1, 0), 0)
    nxt = lambda i: (jnp.minimum((i + 1) * nblk, n_halo_blocks - 1), 0)
    return prev, nxt


def _mix(attn, hglu, sg, x, g1, wa, wc, wo, dww, dwb, lng, lnb):
    tm = TM_MIX
    row = lambda i: (i, 0)
    prev, nxt = _halo_maps(tm, CONV_HALO)
    cvec = _const_spec((1, CONV_DIM))
    ext = tm + 2 * CONV_HALO
    return pl.pallas_call(
        _mix_kernel,
        out_shape=jax.ShapeDtypeStruct((SEQ, D_MODEL), F32),
        grid=(SEQ // tm,),
        in_specs=[pl.BlockSpec((tm, Q_DIM), row),
                  pl.BlockSpec((tm, CONV_DIM), row),
                  pl.BlockSpec((CONV_HALO, CONV_DIM), prev),
                  pl.BlockSpec((CONV_HALO, CONV_DIM), nxt),
                  pl.BlockSpec((tm, 2 * D_MODEL), row),
                  pl.BlockSpec((tm, D_MODEL), row),
                  _const_spec((1, D_MODEL)),
                  _const_spec((Q_DIM, D_MODEL)), _const_spec((CONV_DIM, D_MODEL)),
                  _const_spec((D_MODEL, D_MODEL)),
                  _const_spec((CONV_WIDTH, CONV_DIM)), cvec, cvec, cvec],
        out_specs=pl.BlockSpec((tm, D_MODEL), row),
        scratch_shapes=[pltpu.VMEM((ext, CONV_DIM), F32),
                        pltpu.VMEM((SUBLANES_F32 - 1, ext, CONV_DIM), F32)],
        compiler_params=_params(("arbitrary",)),
        name="mix",
    )(attn, hglu, hglu, hglu, sg, x, g1, wa, wc, wo, dww, dwb, lng, lnb)


def _ffn_kernel(xm_ref, xp_ref, xn_ref, ng_ref, sc_ref, sh_ref, g2_ref, fg_ref,
                wup_ref, dw_ref, wdn_ref, o_ref, xbuf_ref, abuf_ref, act_ref):
    i = pl.program_id(0)
    tm = xm_ref.shape[0]
    xbuf_ref[0:FFN_HALO, :] = xp_ref[...]
    xbuf_ref[FFN_HALO:FFN_HALO + tm, :] = xm_ref[...]
    xbuf_ref[FFN_HALO + tm:, :] = xn_ref[...]
    h = (_rms(xbuf_ref[...]) * ng_ref[...]) * (1.0 + sc_ref[...]) + sh_ref[...]
    hb = h.astype(BF16)
    last = pl.num_programs(0) - 1
    for off in range(0, FFN_DIM, FFN_CHUNK):
        cols = slice(off, off + FFN_CHUNK)
        a = _dot(hb, wup_ref[:, cols])
        abuf_ref[0:FFN_HALO, cols] = jnp.where(i > 0, a[0:FFN_HALO, :], 0.0)
        abuf_ref[FFN_HALO:FFN_HALO + tm, cols] = a[FFN_HALO:FFN_HALO + tm, :]
        abuf_ref[FFN_HALO + tm:, cols] = jnp.where(i < last, a[FFN_HALO + tm:, :], 0.0)
        b = _dot(hb[FFN_HALO:FFN_HALO + tm, :], wup_ref[:, FFN_DIM + off:FFN_DIM + off + FFN_CHUNK])
        conv = (abuf_ref[FFN_HALO - 1:FFN_HALO - 1 + tm, cols] * dw_ref[0:1, cols]
                + abuf_ref[FFN_HALO:FFN_HALO + tm, cols] * dw_ref[1:2, cols]
                + abuf_ref[FFN_HALO + 1:FFN_HALO + 1 + tm, cols] * dw_ref[2:3, cols]
                + dw_ref[3:4, cols])
        act_ref[:, cols] = (jax.nn.gelu(conv, approximate=True) * b).astype(BF16)
    x2 = xm_ref[...] + g2_ref[...] * _dot(act_ref[...], wdn_ref[...])
    o_ref[...] = _rms(x2) * fg_ref[...]


def _ffn(x1, ng, sc, sh, g2, fg, wup, dw, wdn):
    tm = TM_FFN
    row = lambda i: (i, 0)
    prev, nxt = _halo_maps(tm, FFN_HALO)
    vec = _const_spec((1, D_MODEL))
    ext = tm + 2 * FFN_HALO
    return pl.pallas_call(
        _ffn_kernel,
        out_shape=jax.ShapeDtypeStruct((SEQ, D_MODEL), F32),
        grid=(SEQ // tm,),
        in_specs=[pl.BlockSpec((tm, D_MODEL), row),
                  pl.BlockSpec((FFN_HALO, D_MODEL), prev),
                  pl.BlockSpec((FFN_HALO, D_MODEL), nxt),
                  vec, vec, vec, vec, vec,
                  _const_spec((D_MODEL, 2 * FFN_DIM)),
                  _const_spec((SUBLANES_F32, FFN_DIM)),
                  _const_spec((FFN_DIM, D_MODEL))],
        out_specs=pl.BlockSpec((tm, D_MODEL), row),
        scratch_shapes=[pltpu.VMEM((ext, D_MODEL), F32),
                        pltpu.VMEM((ext, FFN_DIM), F32),
                        pltpu.VMEM((tm, FFN_DIM), BF16)],
        compiler_params=_params(("arbitrary",)),
        name="ffn",
    )(x1, x1, x1, ng, sc, sh, g2, fg, wup, dw, wdn)


def _rope_tables(tm):
    half = HEAD_DIM // 2
    rows = SEQ // GRID_W
    inv_freq = ROPE_THETA ** (-jnp.arange(0, half, 2, dtype=F32) / half)
    row_ang = jnp.arange(rows, dtype=F32)[:, None] * inv_freq
    col_ang = jnp.arange(GRID_W, dtype=F32)[:, None] * inv_freq

    def pair_tables(ang):
        c, s = jnp.cos(ang), jnp.sin(ang)
        return jnp.repeat(c, 2, axis=-1), jnp.stack([-s, s], axis=-1).reshape(ang.shape[0], half)

    row_tabs = [jnp.pad(t, ((0, 0), (0, half))) for t in pair_tables(row_ang)]
    col_tabs = [jnp.tile(jnp.pad(t, ((0, 0), (half, 0))), (tm // GRID_W, 1))
                for t in pair_tables(col_ang)]
    return row_tabs + col_tabs


def kernel(x, c, ctx, c_ctx, w_mod, b_mod, norm1_g, w_in, q_norm_g, k_norm_g, w_attn_out,
           conv_dw_w, conv_dw_b, conv_ln_g, conv_ln_b, w_conv_out, w_out, norm2_g,
           w_up, ffn_dw_w, ffn_dw_b, w_down, final_g):
    assert x.shape == (1, SEQ, D_MODEL) and ctx.shape == (1, CTX_LEN, D_MODEL)
    assert w_mod.shape[0] == 1, "single layer"
    x2d = x[0]
    ctx2d = ctx[0]
    r1 = lambda v: v.reshape(1, -1)

    cvecs = jnp.zeros((SUBLANES_F32, D_MODEL), F32).at[0].set(c[0]).at[1].set(c_ctx)
    mod = _modulation(cvecs, w_mod[0], r1(b_mod[0]))
    sh1, sc1, g1, sh2, sc2, g2 = [r1(mod[0, j * D_MODEL:(j + 1) * D_MODEL]) for j in range(6)]
    csh1, csc1 = [r1(mod[1, j * D_MODEL:(j + 1) * D_MODEL]) for j in range(2)]

    w_in_b = w_in[0].astype(BF16)
    ng1, qg, kg = r1(norm1_g[0]), r1(q_norm_g[0]), r1(k_norm_g[0])
    q, kx, vtx, hglu, sg = _in_proj(x2d, ng1, sc1, sh1, w_in_b, qg, kg,
                                    _rope_tables(TM_PROJ), tm=TM_PROJ)
    kc, vtc = _in_proj(ctx2d, ng1, csc1, csh1, w_in_b, qg, kg, tm=CTX_LEN)
    vtx = vtx.reshape(N_KV_HEADS, HEAD_DIM, SEQ)
    vtc = vtc.reshape(N_KV_HEADS, HEAD_DIM, CTX_LEN)

    bound = SHIFT_MARGIN * HEAD_DIM ** 0.5 * jnp.max(jnp.abs(q_norm_g[0])) * jnp.max(jnp.abs(k_norm_g[0]))
    shifts = lax.cond(bound > SAFE_SHIFT,
                      lambda: _row_max(q, kx, kc),
                      lambda: jnp.full((SEQ, LANES), bound, F32))
    attn = _attention(q, shifts, kx, vtx, kc, vtc)

    x1 = _mix(attn, hglu, sg, x2d, g1,
              w_attn_out[0].astype(BF16), w_conv_out[0].astype(BF16), w_out[0].astype(BF16),
              conv_dw_w[0], r1(conv_dw_b[0]), r1(conv_ln_g[0]), r1(conv_ln_b[0]))

    dw = jnp.concatenate([ffn_dw_w[0], ffn_dw_b[0][None, :],
                          jnp.zeros((SUBLANES_F32 - 4, FFN_DIM), F32)], axis=0)
    out = _ffn(x1, r1(norm2_g[0]), sc2, sh2, g2, r1(final_g),
               w_up[0].astype(BF16), dw, w_down[0].astype(BF16))
    return out[None]
```

```python
import functools

import jax
import jax.numpy as jnp
from jax import lax
from jax.experimental import pallas as pl
from jax.experimental.pallas import tpu as pltpu

D_MODEL = 1024
SEQ = 16384
GRID_W = 64
CTX_LEN = 256
N_HEADS = 8
N_KV_HEADS = 2
HEAD_DIM = 128
ROPE_THETA = 10000.0
CONV_DIM = 512
CONV_WIDTH = 31
FFN_DIM = 2816
EPS = 1e-6
Q_DIM = N_HEADS * HEAD_DIM
KV_DIM = N_KV_HEADS * HEAD_DIM
GROUP = N_HEADS // N_KV_HEADS
U_OFF = Q_DIM + 2 * KV_DIM
G_OFF = U_OFF + 2 * CONV_DIM
IN_DIM = G_OFF + 2 * D_MODEL

F32 = jnp.float32
BF16 = jnp.bfloat16

LANES = 128
SUBLANES_F32 = 8
MXU_DIM = 256
VMEM_LIMIT_BYTES = 56 * 1024 * 1024

TM_PROJ = 512
TQ = 2048
TK = 4096
KP_DIM = 2 * HEAD_DIM
TM_MIX = 512
CONV_HALO = 16
TM_FFN = 512
FFN_HALO = SUBLANES_F32
FFN_CHUNK = MXU_DIM
SAFE_SHIFT = 40.0
SHIFT_MARGIN = 1.02
N_COND_ROWS = 2

assert SEQ % TK == 0 and SEQ % TQ == 0 and SEQ % TM_PROJ == 0 and CTX_LEN % MXU_DIM == 0
assert FFN_DIM % FFN_CHUNK == 0 and CONV_HALO >= CONV_WIDTH // 2


def _dot(a, b):
    return jnp.dot(a, b, preferred_element_type=F32)


def _dot_nt(a, b):
    return lax.dot_general(a, b, (((1,), (1,)), ((), ())), preferred_element_type=F32)


def _rms(x):
    return x * lax.rsqrt(jnp.mean(x * x, axis=-1, keepdims=True) + EPS)


def _const_spec(shape):
    return pl.BlockSpec(shape, lambda *_: (0,) * len(shape), pipeline_mode=pl.Buffered(1))


def _params(sem):
    return pltpu.CompilerParams(dimension_semantics=sem, vmem_limit_bytes=VMEM_LIMIT_BYTES)


def _mod_kernel(c_ref, w_ref, b_ref, o_ref):
    c = c_ref[...]
    st = (c * jax.nn.sigmoid(c)).T
    w = w_ref[...]
    rows = [jnp.sum(st[:, r:r + 1] * w, axis=0, keepdims=True) for r in range(N_COND_ROWS)]
    pad = jnp.zeros((c.shape[0] - N_COND_ROWS, w.shape[1]), F32)
    o_ref[...] = jnp.concatenate(rows + [pad], axis=0) + b_ref[...]


def _modulation(cvecs, w_mod, b_mod):
    rows, n_out = cvecs.shape[0], w_mod.shape[1]
    bn = 1024
    return pl.pallas_call(
        _mod_kernel,
        out_shape=jax.ShapeDtypeStruct((rows, n_out), F32),
        grid=(n_out // bn,),
        in_specs=[pl.BlockSpec((rows, D_MODEL), lambda j: (0, 0)),
                  pl.BlockSpec((D_MODEL, bn), lambda j: (0, j)),
                  pl.BlockSpec((1, bn), lambda j: (0, j))],
        out_specs=pl.BlockSpec((rows, bn), lambda j: (0, j)),
        compiler_params=_params(("arbitrary",)),
        name="modulation",
    )(cvecs, w_mod, b_mod)


def _rope(t, cos2, sin2, even):
    swapped = jnp.where(even, pltpu.roll(t, HEAD_DIM - 1, 1), pltpu.roll(t, 1, 1))
    return t * cos2 + swapped * sin2


def _expand_rope_table(row_ref, col_ref):
    rows = row_ref[...]
    per_token = jnp.broadcast_to(rows[:, None, :], (rows.shape[0], GRID_W, HEAD_DIM))
    return per_token.reshape(col_ref.shape) + col_ref[...]


def _inproj_kernel(x_ref, g_ref, sc_ref, sh_ref, w_ref, qg_ref, kg_ref, *refs, full):
    if full:
        (rcos_ref, rsin_ref, ccos_ref, csin_ref,
         q_ref, k_ref, vt_ref, h_ref, sg_ref) = refs
    else:
        k_ref, vt_ref = refs
    tm = x_ref.shape[0]
    h = (_rms(x_ref[...]) * g_ref[...]) * (1.0 + sc_ref[...]) + sh_ref[...]
    hb = h.astype(BF16)
    lane = lax.broadcasted_iota(jnp.int32, (tm, LANES), 1)
    even = (lane % 2) == 0

    if full:
        cos2 = _expand_rope_table(rcos_ref, ccos_ref)
        sin2 = _expand_rope_table(rsin_ref, csin_ref)
        qa = _dot(hb, w_ref[:, 0:Q_DIM])
        scale = HEAD_DIM ** -0.5
        for hd in range(N_HEADS):
            t = _rms(qa[:, hd * HEAD_DIM:(hd + 1) * HEAD_DIM]) * qg_ref[...]
            q_ref[:, hd * HEAD_DIM:(hd + 1) * HEAD_DIM] = (
                _rope(t, cos2, sin2, even) * scale).astype(BF16)

    ka = _dot(hb, w_ref[:, Q_DIM:Q_DIM + KV_DIM])
    ones_col = jnp.where(lane == 0, 1.0, 0.0).astype(BF16)
    for g in range(N_KV_HEADS):
        t = _rms(ka[:, g * HEAD_DIM:(g + 1) * HEAD_DIM]) * kg_ref[...]
        if full:
            t = _rope(t, cos2, sin2, even)
        k_ref[g, :, 0:HEAD_DIM] = t.astype(BF16)
        k_ref[g, :, HEAD_DIM:] = ones_col

    va = _dot(hb, w_ref[:, Q_DIM + KV_DIM:U_OFF])
    vt_ref[...] = va.T.astype(BF16)

    if full:
        ua = _dot(hb, w_ref[:, U_OFF:U_OFF + CONV_DIM])
        ub = _dot(hb, w_ref[:, U_OFF + CONV_DIM:G_OFF])
        h_ref[...] = ua * jax.nn.sigmoid(ub)
        sg_ref[...] = jax.nn.sigmoid(_dot(hb, w_ref[:, G_OFF:IN_DIM]))


def _in_proj(x, g, sc, sh, w_in_b, qg, kg, rope=None, *, tm):
    full = rope is not None
    n = x.shape[0]
    row = lambda i: (i, 0)
    vec = _const_spec((1, D_MODEL))
    hvec = _const_spec((1, HEAD_DIM))
    in_specs = [pl.BlockSpec((tm, D_MODEL), row), vec, vec, vec,
                _const_spec((D_MODEL, IN_DIM)), hvec, hvec]
    operands = [x, g, sc, sh, w_in_b, qg, kg]
    if full:
        grid_rows = tm // GRID_W
        assert grid_rows % SUBLANES_F32 == 0 and rope[2].shape == (tm, HEAD_DIM)
        row_tab = pl.BlockSpec((grid_rows, HEAD_DIM), row)
        in_specs += [row_tab, row_tab, _const_spec((tm, HEAD_DIM)), _const_spec((tm, HEAD_DIM))]
        operands += list(rope)
    k_out = (jax.ShapeDtypeStruct((N_KV_HEADS, n, KP_DIM), BF16),
             pl.BlockSpec((N_KV_HEADS, tm, KP_DIM), lambda i: (0, i, 0)))
    vt_out = (jax.ShapeDtypeStruct((KV_DIM, n), BF16), pl.BlockSpec((KV_DIM, tm), lambda i: (0, i)))
    if full:
        outs = [(jax.ShapeDtypeStruct((n, Q_DIM), BF16), pl.BlockSpec((tm, Q_DIM), row)),
                k_out, vt_out,
                (jax.ShapeDtypeStruct((n, CONV_DIM), F32), pl.BlockSpec((tm, CONV_DIM), row)),
                (jax.ShapeDtypeStruct((n, 2 * D_MODEL), F32), pl.BlockSpec((tm, 2 * D_MODEL), row))]
    else:
        outs = [k_out, vt_out]
    return pl.pallas_call(
        functools.partial(_inproj_kernel, full=full),
        out_shape=[o[0] for o in outs],
        grid=(n // tm,),
        in_specs=in_specs,
        out_specs=[o[1] for o in outs],
        compiler_params=_params(("arbitrary",)),
        name="in_proj" if full else "ctx_proj",
    )(*operands)


def _head_column(ref, hd):
    lane = lax.broadcasted_iota(jnp.int32, ref.shape, 1)
    return jnp.sum(jnp.where(lane == hd, ref[...], 0.0), axis=-1, keepdims=True)


def _shifted_queries(q_ref, shift):
    lane = lax.broadcasted_iota(jnp.int32, (q_ref.shape[0], LANES), 1)
    shift_col = jnp.where(lane == 0, -shift, 0.0).astype(BF16)
    return jnp.concatenate([q_ref[...], shift_col], axis=1)


def _attn_kernel(q_ref, m_ref, kx_ref, vtx_ref, kc_ref, vtc_ref, o_ref):
    qp = _shifted_queries(q_ref, _head_column(m_ref, pl.program_id(0)))

    def chunk(k_blk, vt_blk):
        p = jnp.exp(_dot_nt(k_blk, qp))
        psum = jnp.sum(p.reshape(-1, SUBLANES_F32, TQ), axis=0)
        return _dot(vt_blk, p.astype(BF16)), psum

    acc, psum = chunk(kc_ref[...], vtc_ref[...])
    for c in range(SEQ // TK):
        a, s = chunk(kx_ref[c * TK:(c + 1) * TK, :], vtx_ref[:, c * TK:(c + 1) * TK])
        acc = acc + a
        psum = psum + s
    out_t = acc / jnp.sum(psum, axis=0, keepdims=True)
    o_ref[...] = out_t.T.astype(BF16)


def _kv_specs():
    kv = lambda h, i: (h // GROUP, 0, 0)
    one = pl.Buffered(1)
    return [pl.BlockSpec((None, SEQ, KP_DIM), kv, pipeline_mode=one),
            pl.BlockSpec((None, HEAD_DIM, SEQ), kv, pipeline_mode=one),
            pl.BlockSpec((None, CTX_LEN, KP_DIM), kv, pipeline_mode=one),
            pl.BlockSpec((None, HEAD_DIM, CTX_LEN), kv, pipeline_mode=one)]


def _attention(q, m, kx, vtx, kc, vtc):
    return pl.pallas_call(
        _attn_kernel,
        out_shape=jax.ShapeDtypeStruct((SEQ, Q_DIM), BF16),
        grid=(N_HEADS, SEQ // TQ),
        in_specs=[pl.BlockSpec((TQ, HEAD_DIM), lambda h, i: (i, h)),
                  pl.BlockSpec((TQ, LANES), lambda h, i: (i, 0))] + _kv_specs(),
        out_specs=pl.BlockSpec((TQ, HEAD_DIM), lambda h, i: (i, h)),
        compiler_params=_params(("arbitrary", "arbitrary")),
        name="attention",
    )(q, m, kx, vtx, kc, vtc)


def _rowmax_kernel(q_ref, kx_ref, kc_ref, o_ref):
    hd = pl.program_id(0)
    qp = _shifted_queries(q_ref, jnp.zeros((q_ref.shape[0], 1), F32))
    m = jnp.max(_dot_nt(qp, kc_ref[...]), axis=-1, keepdims=True)

    def body(c, m):
        k_blk = kx_ref[pl.ds(pl.multiple_of(c * TK, TK), TK), :]
        return jnp.maximum(m, jnp.max(_dot_nt(qp, k_blk), axis=-1, keepdims=True))

    m = lax.fori_loop(0, SEQ // TK, body, m)
    lane = lax.broadcasted_iota(jnp.int32, o_ref.shape[1:], 1)
    o_ref[0] = jnp.where(lane == hd, m, 0.0)


def _row_max(q, kx, kc):
    tq = 256
    kv = lambda h, i: (h // GROUP, 0, 0)
    per_head = pl.pallas_call(
        _rowmax_kernel,
        out_shape=jax.ShapeDtypeStruct((N_HEADS, SEQ, LANES), F32),
        grid=(N_HEADS, SEQ // tq),
        in_specs=[pl.BlockSpec((tq, HEAD_DIM), lambda h, i: (i, h)),
                  pl.BlockSpec((None, SEQ, KP_DIM), kv, pipeline_mode=pl.Buffered(1)),
                  pl.BlockSpec((None, CTX_LEN, KP_DIM), kv, pipeline_mode=pl.Buffered(1))],
        out_specs=pl.BlockSpec((1, tq, LANES), lambda h, i: (h, i, 0)),
        compiler_params=_params(("arbitrary", "arbitrary")),
        name="row_max",
    )(q, kx, kc)
    return jnp.sum(per_head, axis=0)


def _mix_kernel(attn_ref, hm_ref, hp_ref, hn_ref, sg_ref, x_ref, g1_ref,
                wa_ref, wc_ref, wo_ref, dww_ref, dwb_ref, lng_ref, lnb_ref,
                o_ref, hbuf_ref, sbuf_ref):
    i = pl.program_id(0)
    last = pl.num_programs(0) - 1
    tm = x_ref.shape[0]
    hbuf_ref[0:CONV_HALO, :] = jnp.where(i > 0, hp_ref[...], 0.0)
    hbuf_ref[CONV_HALO:CONV_HALO + tm, :] = hm_ref[...]
    hbuf_ref[CONV_HALO + tm:, :] = jnp.where(i < last, hn_ref[...], 0.0)
    span = tm + 2 * CONV_HALO - SUBLANES_F32
    for r in range(1, SUBLANES_F32):
        sbuf_ref[r - 1, 0:span, :] = hbuf_ref[r:r + span, :]

    first = CONV_HALO - CONV_WIDTH // 2
    conv = None
    for k in range(CONV_WIDTH):
        a, r = divmod(first + k, SUBLANES_F32)
        src = hbuf_ref if r == 0 else sbuf_ref.at[r - 1]
        term = src[a * SUBLANES_F32:a * SUBLANES_F32 + tm, :] * dww_ref[k:k + 1, :]
        conv = term + dwb_ref[...] if conv is None else conv + term
    mu = jnp.mean(conv, axis=-1, keepdims=True)
    cen = conv - mu
    var = jnp.mean(cen * cen, axis=-1, keepdims=True)
    ln = cen * lax.rsqrt(var + EPS) * lng_ref[...] + lnb_ref[...]
    act = ln * jax.nn.sigmoid(ln)
    y_b = _dot(act.astype(BF16), wc_ref[...])
    y_a = _dot(attn_ref[...], wa_ref[...])
    merged = sg_ref[:, 0:D_MODEL] * y_a + sg_ref[:, D_MODEL:] * y_b
    o_ref[...] = x_ref[...] + g1_ref[...] * _dot(merged.astype(BF16), wo_ref[...])


def _halo_maps(tm, halo):
    nblk = tm // halo
    n_halo_blocks = SEQ // halo
    prev = lambda i: (jnp.maximum(i * nblk - 1, 0), 0)
    nxt = lambda i: (jnp.minimum((i + 1) * nblk, n_halo_blocks - 1), 0)
    return prev, nxt


def _mix(attn, hglu, sg, x, g1, wa, wc, wo, dww, dwb, lng, lnb):
    tm = TM_MIX
    row = lambda i: (i, 0)
    prev, nxt = _halo_maps(tm, CONV_HALO)
    cvec = _const_spec((1, CONV_DIM))
    ext = tm + 2 * CONV_HALO
    return pl.pallas_call(
        _mix_kernel,
        out_shape=jax.ShapeDtypeStruct((SEQ, D_MODEL), F32),
        grid=(SEQ // tm,),
        in_specs=[pl.BlockSpec((tm, Q_DIM), row),
                  pl.BlockSpec((tm, CONV_DIM), row),
                  pl.BlockSpec((CONV_HALO, CONV_DIM), prev),
                  pl.BlockSpec((CONV_HALO, CONV_DIM), nxt),
                  pl.BlockSpec((tm, 2 * D_MODEL), row),
                  pl.BlockSpec((tm, D_MODEL), row),
                  _const_spec((1, D_MODEL)),
                  _const_spec((Q_DIM, D_MODEL)), _const_spec((CONV_DIM, D_MODEL)),
                  _const_spec((D_MODEL, D_MODEL)),
                  _const_spec((CONV_WIDTH, CONV_DIM)), cvec, cvec, cvec],
        out_specs=pl.BlockSpec((tm, D_MODEL), row),
        scratch_shapes=[pltpu.VMEM((ext, CONV_DIM), F32),
                        pltpu.VMEM((SUBLANES_F32 - 1, ext, CONV_DIM), F32)],
        compiler_params=_params(("arbitrary",)),
        name="mix",
    )(attn, hglu, hglu, hglu, sg, x, g1, wa, wc, wo, dww, dwb, lng, lnb)


def _ffn_kernel(xm_ref, xp_ref, xn_ref, ng_ref, sc_ref, sh_ref, g2_ref, fg_ref,
                wup_ref, dw_ref, wdn_ref, o_ref, xbuf_ref, abuf_ref, act_ref):
    i = pl.program_id(0)
    tm = xm_ref.shape[0]
    xbuf_ref[0:FFN_HALO, :] = xp_ref[...]
    xbuf_ref[FFN_HALO:FFN_HALO + tm, :] = xm_ref[...]
    xbuf_ref[FFN_HALO + tm:, :] = xn_ref[...]
    h = (_rms(xbuf_ref[...]) * ng_ref[...]) * (1.0 + sc_ref[...]) + sh_ref[...]
    hb = h.astype(BF16)
    last = pl.num_programs(0) - 1
    for off in range(0, FFN_DIM, FFN_CHUNK):
        cols = slice(off, off + FFN_CHUNK)
        a = _dot(hb, wup_ref[:, cols])
        abuf_ref[0:FFN_HALO, cols] = jnp.where(i > 0, a[0:FFN_HALO, :], 0.0)
        abuf_ref[FFN_HALO:FFN_HALO + tm, cols] = a[FFN_HALO:FFN_HALO + tm, :]
        abuf_ref[FFN_HALO + tm:, cols] = jnp.where(i < last, a[FFN_HALO + tm:, :], 0.0)
        b = _dot(hb[FFN_HALO:FFN_HALO + tm, :], wup_ref[:, FFN_DIM + off:FFN_DIM + off + FFN_CHUNK])
        conv = (abuf_ref[FFN_HALO - 1:FFN_HALO - 1 + tm, cols] * dw_ref[0:1, cols]
                + abuf_ref[FFN_HALO:FFN_HALO + tm, cols] * dw_ref[1:2, cols]
                + abuf_ref[FFN_HALO + 1:FFN_HALO + 1 + tm, cols] * dw_ref[2:3, cols]
                + dw_ref[3:4, cols])
        act_ref[:, cols] = (jax.nn.gelu(conv, approximate=True) * b).astype(BF16)
    x2 = xm_ref[...] + g2_ref[...] * _dot(act_ref[...], wdn_ref[...].astype(BF16))
    o_ref[...] = _rms(x2) * fg_ref[...]


def _ffn(x1, ng, sc, sh, g2, fg, wup, dw, wdn):
    tm = TM_FFN
    row = lambda i: (i, 0)
    prev, nxt = _halo_maps(tm, FFN_HALO)
    vec = _const_spec((1, D_MODEL))
    ext = tm + 2 * FFN_HALO
    return pl.pallas_call(
        _ffn_kernel,
        out_shape=jax.ShapeDtypeStruct((SEQ, D_MODEL), F32),
        grid=(SEQ // tm,),
        in_specs=[pl.BlockSpec((tm, D_MODEL), row),
                  pl.BlockSpec((FFN_HALO, D_MODEL), prev),
                  pl.BlockSpec((FFN_HALO, D_MODEL), nxt),
                  vec, vec, vec, vec, vec,
                  _const_spec((D_MODEL, 2 * FFN_DIM)),
                  _const_spec((SUBLANES_F32, FFN_DIM)),
                  _const_spec((FFN_DIM, D_MODEL))],
        out_specs=pl.BlockSpec((tm, D_MODEL), row),
        scratch_shapes=[pltpu.VMEM((ext, D_MODEL), F32),
                        pltpu.VMEM((ext, FFN_DIM), F32),
                        pltpu.VMEM((tm, FFN_DIM), BF16)],
        compiler_params=_params(("arbitrary",)),
        name="ffn",
    )(x1, x1, x1, ng, sc, sh, g2, fg, wup, dw, wdn)


def _rope_tables(tm):
    half = HEAD_DIM // 2
    rows = SEQ // GRID_W
    inv_freq = ROPE_THETA ** (-jnp.arange(0, half, 2, dtype=F32) / half)
    row_ang = jnp.arange(rows, dtype=F32)[:, None] * inv_freq
    col_ang = jnp.arange(GRID_W, dtype=F32)[:, None] * inv_freq

    def pair_tables(ang):
        c, s = jnp.cos(ang), jnp.sin(ang)
        return jnp.repeat(c, 2, axis=-1), jnp.stack([-s, s], axis=-1).reshape(ang.shape[0], half)

    row_tabs = [jnp.pad(t, ((0, 0), (0, half))) for t in pair_tables(row_ang)]
    col_tabs = [jnp.tile(jnp.pad(t, ((0, 0), (half, 0))), (tm // GRID_W, 1))
                for t in pair_tables(col_ang)]
    return row_tabs + col_tabs


def kernel(x, c, ctx, c_ctx, w_mod, b_mod, norm1_g, w_in, q_norm_g, k_norm_g, w_attn_out,
           conv_dw_w, conv_dw_b, conv_ln_g, conv_ln_b, w_conv_out, w_out, norm2_g,
           w_up, ffn_dw_w, ffn_dw_b, w_down, final_g):
    assert x.shape == (1, SEQ, D_MODEL) and ctx.shape == (1, CTX_LEN, D_MODEL)
    assert w_mod.shape[0] == 1, "single layer"
    x2d = x[0]
    ctx2d = ctx[0]
    r1 = lambda v: v.reshape(1, -1)

    cvecs = jnp.zeros((SUBLANES_F32, D_MODEL), F32).at[0].set(c[0]).at[1].set(c_ctx)
    mod = _modulation(cvecs, w_mod[0], r1(b_mod[0]))
    sh1, sc1, g1, sh2, sc2, g2 = [r1(mod[0, j * D_MODEL:(j + 1) * D_MODEL]) for j in range(6)]
    csh1, csc1 = [r1(mod[1, j * D_MODEL:(j + 1) * D_MODEL]) for j in range(2)]

    w_in_b = w_in[0].astype(BF16)
    ng1, qg, kg = r1(norm1_g[0]), r1(q_norm_g[0]), r1(k_norm_g[0])
    q, kx, vtx, hglu, sg = _in_proj(x2d, ng1, sc1, sh1, w_in_b, qg, kg,
                                    _rope_tables(TM_PROJ), tm=TM_PROJ)
    kc, vtc = _in_proj(ctx2d, ng1, csc1, csh1, w_in_b, qg, kg, tm=CTX_LEN)
    vtx = vtx.reshape(N_KV_HEADS, HEAD_DIM, SEQ)
    vtc = vtc.reshape(N_KV_HEADS, HEAD_DIM, CTX_LEN)

    bound = SHIFT_MARGIN * HEAD_DIM ** 0.5 * jnp.max(jnp.abs(q_norm_g[0])) * jnp.max(jnp.abs(k_norm_g[0]))
    shifts = lax.cond(bound > SAFE_SHIFT,
                      lambda: _row_max(q, kx, kc),
                      lambda: jnp.full((SEQ, LANES), bound, F32))
    attn = _attention(q, shifts, kx, vtx, kc, vtc)

    x1 = _mix(attn, hglu, sg, x2d, g1,
              w_attn_out[0].astype(BF16), w_conv_out[0].astype(BF16), w_out[0].astype(BF16),
              conv_dw_w[0], r1(conv_dw_b[0]), r1(conv_ln_g[0]), r1(conv_ln_b[0]))

    dw = jnp.concatenate([ffn_dw_w[0], ffn_dw_b[0][None, :],
                          jnp.zeros((SUBLANES_F32 - 4, FFN_DIM), F32)], axis=0)
    out = _ffn(x1, r1(norm2_g[0]), sc2, sh2, g2, r1(final_g),
               w_up[0].astype(BF16), dw, w_down[0])
    return out[None]
```
